```python
import math
import jax, jax.numpy as jnp
from jax import lax
import numpy as np

D_MODEL = 1024
BATCH = 4
SEQ = 4096
DEPTH = 2

MEM_LEN = 256
HEAD_DIM = 64
CROSS_HEADS = 4
CROSS_WIDTH = CROSS_HEADS * HEAD_DIM
TOKEN_WIDTH = D_MODEL - CROSS_WIDTH
FOX_HEADS = TOKEN_WIDTH // HEAD_DIM
CONV_CHANNELS = TOKEN_WIDTH
CONV_WIDTH = 31
FFN_DIM = 256 * math.ceil(8 * D_MODEL / 3 / 256)
FFN_CONV_WIDTH = 3
Q_BLOCK = 128
N_FOX_LAYERS = (DEPTH + 1) // 2
N_CONV_LAYERS = DEPTH // 2
FOX_IN = 3 * TOKEN_WIDTH + FOX_HEADS + CROSS_WIDTH
CONV_IN = 2 * CONV_CHANNELS + CROSS_WIDTH
EPS = 1e-6
NEG = -1e30

kernel_name = "fox_conformer_interleaved_hybrid"


def rmsnorm(x, g):
    xf = x.astype(jnp.float32)
    y = xf * lax.rsqrt(jnp.mean(xf * xf, axis=-1, keepdims=True) + EPS)
    return (y * g.astype(jnp.float32)).astype(x.dtype)


def layernorm(x, g, b):
    xf = x.astype(jnp.float32)
    mu = jnp.mean(xf, axis=-1, keepdims=True)
    xc = xf - mu
    y = xc * lax.rsqrt(jnp.mean(xc * xc, axis=-1, keepdims=True) + EPS)
    return (y * g.astype(jnp.float32) + b.astype(jnp.float32)).astype(x.dtype)


def causal_dwconv(x, w):
    k_w, c = w.shape
    return lax.conv_general_dilated(
        x, w[:, None, :].astype(x.dtype), window_strides=(1,), padding=[(k_w - 1, 0)],
        dimension_numbers=('NWC', 'WIO', 'NWC'), feature_group_count=c)


def cross_attention(q, k, v):
    scale = 1.0 / math.sqrt(q.shape[-1])
    s = jnp.einsum('bshd,bmhd->bhsm', q, k).astype(jnp.float32) * scale
    p = jax.nn.softmax(s, axis=-1)
    return jnp.einsum('bhsm,bmhd->bshd', p.astype(v.dtype), v)


def fox_attention(q, k, v, log_f):
    scale = 1.0 / math.sqrt(q.shape[-1])
    c = jnp.transpose(jnp.cumsum(log_f, axis=1), (0, 2, 1))
    n_blocks = q.shape[1] // Q_BLOCK
    outs = []
    for blk in range(n_blocks):
        qs = blk * Q_BLOCK
        qe = qs + Q_BLOCK
        s = jnp.einsum('bqhd,bkhd->bhqk', q[:, qs:qe], k[:, :qe]).astype(jnp.float32) * scale
        bias = c[:, :, qs:qe, None] - c[:, :, None, :qe]
        mask = jnp.arange(qe)[None, :] <= (qs + jnp.arange(Q_BLOCK))[:, None]
        s = jnp.where(mask, s + bias, NEG)
        p = jax.nn.softmax(s, axis=-1)
        outs.append(jnp.einsum('bhqk,bkhd->bqhd', p.astype(v.dtype), v[:, :qe]))
    return jnp.concatenate(outs, axis=1)


def setup_inputs(seed: int = 0) -> dict:
    key = jax.random.key(seed)
    ks = jax.random.split(key, 24)
    f32 = jnp.float32

    def nrm(k, shape, scale):
        return jax.random.normal(k, shape, f32) * scale

    def gain(k, shape):
        return 1.0 + 0.1 * jax.random.normal(k, shape, f32)

    return {
        "x": nrm(ks[0], (BATCH, SEQ, D_MODEL), 1.0),
        "mem": nrm(ks[1], (BATCH, MEM_LEN, D_MODEL), 1.0),
        "mem_norm_g": gain(ks[2], (D_MODEL,)),
        "mem_w_kv": nrm(ks[3], (D_MODEL, 2 * CROSS_WIDTH), D_MODEL ** -0.5),
        "mix_norm_g": gain(ks[4], (DEPTH, D_MODEL)),
        "mix_w_out": nrm(ks[5], (DEPTH, D_MODEL, D_MODEL), D_MODEL ** -0.5),
        "cross_q_g": gain(ks[6], (DEPTH, HEAD_DIM)),
        "cross_k_g": gain(ks[7], (DEPTH, HEAD_DIM)),
        "fox_w_in": nrm(ks[8], (N_FOX_LAYERS, D_MODEL, FOX_IN), D_MODEL ** -0.5),
        "fox_b_f": 3.0 + 0.5 * jax.random.normal(ks[9], (N_FOX_LAYERS, FOX_HEADS), f32),
        "fox_q_g": gain(ks[10], (N_FOX_LAYERS, HEAD_DIM)),
        "fox_k_g": gain(ks[11], (N_FOX_LAYERS, HEAD_DIM)),
        "conv_w_in": nrm(ks[12], (N_CONV_LAYERS, D_MODEL, CONV_IN), D_MODEL ** -0.5),
        "conv_dw": nrm(ks[13], (N_CONV_LAYERS, CONV_WIDTH, CONV_CHANNELS), CONV_WIDTH ** -0.5),
        "conv_dw_b": nrm(ks[14], (N_CONV_LAYERS, CONV_CHANNELS), 0.02),
        "conv_ln_g": gain(ks[15], (N_CONV_LAYERS, CONV_CHANNELS)),
        "conv_ln_b": nrm(ks[16], (N_CONV_LAYERS, CONV_CHANNELS), 0.02),
        "ffn_norm_g": gain(ks[17], (DEPTH, D_MODEL)),
        "ffn_w_up": nrm(ks[18], (DEPTH, D_MODEL, 2 * FFN_DIM), D_MODEL ** -0.5),
        "ffn_conv": nrm(ks[19], (DEPTH, FFN_CONV_WIDTH, 2 * FFN_DIM), FFN_CONV_WIDTH ** -0.5),
        "ffn_w_down": nrm(ks[20], (DEPTH, FFN_DIM, D_MODEL), FFN_DIM ** -0.5),
    }


def reference(x, mem, mem_norm_g, mem_w_kv, mix_norm_g, mix_w_out, cross_q_g, cross_k_g,
              fox_w_in, fox_b_f, fox_q_g, fox_k_g,
              conv_w_in, conv_dw, conv_dw_b, conv_ln_g, conv_ln_b,
              ffn_norm_g, ffn_w_up, ffn_conv, ffn_w_down):
    b, s, _ = x.shape
    m = mem.shape[1]

    mem_kv = rmsnorm(mem, mem_norm_g) @ mem_w_kv
    k_mem, v_mem = jnp.split(mem_kv, 2, axis=-1)
    k_mem = k_mem.reshape(b, m, CROSS_HEADS, HEAD_DIM)
    v_mem = v_mem.reshape(b, m, CROSS_HEADS, HEAD_DIM)

    for i in range(DEPTH):
        j = i // 2
        h = rmsnorm(x, mix_norm_g[i])
        if i % 2 == 0:
            proj = h @ fox_w_in[j]
            q, k, v, f_logit, cq = jnp.split(
                proj, [TOKEN_WIDTH, 2 * TOKEN_WIDTH, 3 * TOKEN_WIDTH,
                       3 * TOKEN_WIDTH + FOX_HEADS], axis=-1)
            q = rmsnorm(q.reshape(b, s, FOX_HEADS, HEAD_DIM), fox_q_g[j])
            k = rmsnorm(k.reshape(b, s, FOX_HEADS, HEAD_DIM), fox_k_g[j])
            v = v.reshape(b, s, FOX_HEADS, HEAD_DIM)
            log_f = jax.nn.log_sigmoid(f_logit.astype(jnp.float32)
                                       + fox_b_f[j].astype(jnp.float32))
            tok = fox_attention(q, k, v, log_f).reshape(b, s, TOKEN_WIDTH)
        else:
            proj = h @ conv_w_in[j]
            a, gate, cq = jnp.split(proj, [CONV_CHANNELS, 2 * CONV_CHANNELS], axis=-1)
            u = a * jax.nn.sigmoid(gate)
            u = causal_dwconv(u, conv_dw[j]) + conv_dw_b[j]
            u = layernorm(u, conv_ln_g[j], conv_ln_b[j])
            tok = jax.nn.silu(u)
        cq = rmsnorm(cq.reshape(b, s, CROSS_HEADS, HEAD_DIM), cross_q_g[i])
        ck = rmsnorm(k_mem, cross_k_g[i])
        cross = cross_attention(cq, ck, v_mem).reshape(b, s, CROSS_WIDTH)
        x = x + jnp.concatenate([tok, cross], axis=-1) @ mix_w_out[i]

        h = rmsnorm(x, ffn_norm_g[i])
        u = causal_dwconv(h @ ffn_w_up[i], ffn_conv[i])
        ua, ug = jnp.split(u, 2, axis=-1)
        x = x + (jax.nn.silu(ug) * ua) @ ffn_w_down[i]
    return x
```

```python
import functools
import math

import numpy as np
import jax
import jax.numpy as jnp
from jax import lax
from jax.experimental import pallas as pl
from jax.experimental.pallas import tpu as pltpu

F32 = jnp.float32
BF16 = jnp.bfloat16

D_MODEL = 1024
HEAD_DIM = 64
CROSS_HEADS = 4
CROSS_WIDTH = CROSS_HEADS * HEAD_DIM
TOKEN_WIDTH = D_MODEL - CROSS_WIDTH
FOX_HEADS = TOKEN_WIDTH // HEAD_DIM
HEAD_PAIRS = FOX_HEADS // 2
CONV_WIDTH = 31
FFN_DIM = 2816
FFN_CONV_WIDTH = 3
EPS = 1e-6
NEG = -1e30
LOG2E = 1.4426950408889634
QK_SCALE = 1.0 / math.sqrt(HEAD_DIM)

LANES = 128
SUBLANES = 8
MXU_DIM = 256
PAIR_AUG = 2 * LANES
VMEM_LIMIT = 56 * 1024 * 1024

ROW_BLOCK = 512
ATTN_BLOCK = 512
FFN_CHUNK = 256
CUMSUM_BLOCK = 256
CONV_HALO = 32


def _dot(a, b):
    return jnp.dot(a, b, preferred_element_type=F32)


def _dot_nt(a, b):
    return lax.dot_general(a, b, (((1,), (1,)), ((), ())), preferred_element_type=F32)


def _split2(x):
    hi = x.astype(BF16)
    lo = (x - hi.astype(F32)).astype(BF16)
    return hi, lo


def _split3(x):
    t1 = x.astype(BF16)
    r1 = x - t1.astype(F32)
    t2 = r1.astype(BF16)
    t3 = (r1 - t2.astype(F32)).astype(BF16)
    return t1, t2, t3


def _rms_scale(x):
    return lax.rsqrt(jnp.mean(x * x, axis=-1, keepdims=True) + EPS)


def _head_rms_scale(x, gmat):
    x2 = x * x
    hi, lo = _split2(x2)
    outs = []
    for c in range(x.shape[1] // MXU_DIM):
        sl = slice(c * MXU_DIM, (c + 1) * MXU_DIM)
        outs.append(_dot(hi[:, sl], gmat) + _dot(lo[:, sl], gmat))
    ss = outs[0] if len(outs) == 1 else jnp.concatenate(outs, axis=1)
    return lax.rsqrt(ss * (1.0 / HEAD_DIM) + EPS)


def _log_sigmoid(z):
    return jnp.minimum(z, 0.0) - jnp.log1p(jnp.exp(-jnp.abs(z)))


def _silu(z):
    return z * jax.nn.sigmoid(z)


def _mem_kernel(mem_ref, g_ref, wkv_ref, ckg_ref, gmat_ref, ck_ref, v_ref):
    x = mem_ref[0]
    h = (x * _rms_scale(x) * g_ref[...]).astype(BF16)
    kv = _dot(h, wkv_ref[...])
    k = kv[:, :CROSS_WIDTH]
    v_ref[0] = kv[:, CROSS_WIDTH:].astype(BF16)
    kn = k * _head_rms_scale(k, gmat_ref[...])
    for i in range(ck_ref.shape[0]):
        ck_ref[i, 0] = (kn * ckg_ref[i]).astype(BF16)


def _mem_call(mem, mem_norm_g, wkv, ckg, gmat):
    b, m, _ = mem.shape
    depth = ckg.shape[0]
    return pl.pallas_call(
        _mem_kernel,
        grid=(b,),
        in_specs=[
            pl.BlockSpec((1, m, D_MODEL), lambda i: (i, 0, 0)),
            pl.BlockSpec((1, D_MODEL), lambda i: (0, 0)),
            pl.BlockSpec((D_MODEL, 2 * CROSS_WIDTH), lambda i: (0, 0)),
            pl.BlockSpec((depth, 1, CROSS_WIDTH), lambda i: (0, 0, 0)),
            pl.BlockSpec((MXU_DIM, MXU_DIM), lambda i: (0, 0)),
        ],
        out_specs=[
            pl.BlockSpec((depth, 1, m, CROSS_WIDTH), lambda i: (0, i, 0, 0)),
            pl.BlockSpec((1, m, CROSS_WIDTH), lambda i: (i, 0, 0)),
        ],
        out_shape=[
            jax.ShapeDtypeStruct((depth, b, m, CROSS_WIDTH), BF16),
            jax.ShapeDtypeStruct((b, m, CROSS_WIDTH), BF16),
        ],
        compiler_params=pltpu.CompilerParams(
            dimension_semantics=("arbitrary",), vmem_limit_bytes=VMEM_LIMIT),
        name="mem_kv",
    )(mem, mem_norm_g, wkv, ckg, gmat)


def _fox_in_kernel(x_ref, g_ref, wqkv_ref, wfc_ref, bf_ref, qg_ref, kg_ref, cqg_ref,
                   gmat_ref, tri_ref, pq_ref, pk_ref, oneq_ref, onek_ref,
                   qa_ref, ka_ref, v_ref, cq_ref, carry_ref, *, blocks_per_seq):
    i = pl.program_id(0)

    @pl.when(i % blocks_per_seq == 0)
    def _():
        carry_ref[...] = jnp.zeros_like(carry_ref)

    x = x_ref[...]
    tm = x.shape[0]
    h = (x * _rms_scale(x) * g_ref[...]).astype(BF16)
    gmat = gmat_ref[...]

    q = _dot(h, wqkv_ref[:, 0:TOKEN_WIDTH])
    qn = (q * _head_rms_scale(q, gmat) * qg_ref[...]).astype(BF16)
    k = _dot(h, wqkv_ref[:, TOKEN_WIDTH:2 * TOKEN_WIDTH])
    kn = (k * _head_rms_scale(k, gmat) * kg_ref[...]).astype(BF16)
    v_ref[...] = _dot(h, wqkv_ref[:, 2 * TOKEN_WIDTH:3 * TOKEN_WIDTH]).astype(BF16)

    fc = _dot(h, wfc_ref[...])
    cq = fc[:, :CROSS_WIDTH]
    cq_ref[...] = (cq * _head_rms_scale(cq, gmat) * cqg_ref[...]).astype(BF16)

    lf = _log_sigmoid(fc[:, CROSS_WIDTH:] + bf_ref[...]) * LOG2E
    tri = tri_ref[...]
    carry = carry_ref[0:1, :]
    cs = []
    for sb in range(tm // CUMSUM_BLOCK):
        t1, t2, t3 = _split3(lf[sb * CUMSUM_BLOCK:(sb + 1) * CUMSUM_BLOCK])
        c_sb = (_dot(tri, t1) + _dot(tri, t2)) + _dot(tri, t3) + carry
        carry = c_sb[CUMSUM_BLOCK - 1:CUMSUM_BLOCK, :]
        cs.append(c_sb)
    carry_ref[0:1, :] = carry
    c = jnp.concatenate(cs, axis=0)
    c3 = jnp.concatenate(_split3(c), axis=1)
    qext = (_dot(c3, pq_ref[...]) + oneq_ref[...]).astype(BF16)
    kext = (_dot(c3, pk_ref[...]) + onek_ref[...]).astype(BF16)

    for p in range(HEAD_PAIRS):
        src = slice(p * LANES, (p + 1) * LANES)
        qa_ref[:, p * PAIR_AUG:p * PAIR_AUG + LANES] = qn[:, src]
        qa_ref[:, p * PAIR_AUG + LANES:(p + 1) * PAIR_AUG] = qext[:, src]
        ka_ref[:, p * PAIR_AUG:p * PAIR_AUG + LANES] = kn[:, src]
        ka_ref[:, p * PAIR_AUG + LANES:(p + 1) * PAIR_AUG] = kext[:, src]


def _bias_lane_constants():
    pq = np.zeros((3 * LANES, TOKEN_WIDTH), np.float32)
    pk = np.zeros((3 * LANES, TOKEN_WIDTH), np.float32)
    oneq = np.zeros((1, TOKEN_WIDTH), np.float32)
    onek = np.zeros((1, TOKEN_WIDTH), np.float32)
    for p in range(HEAD_PAIRS):
        for half in range(2):
            head = 2 * p + half
            base = p * LANES + 6 * half
            for t in range(3):
                pq[t * LANES + head, base + t] = 1.0
                pk[t * LANES + head, base + 3 + t] = -1.0
                oneq[0, base + 3 + t] = 1.0
                onek[0, base + t] = 1.0
    return pq, pk, oneq, onek


def _const_spec(shape):
    nd = len(shape)
    return pl.BlockSpec(shape, lambda i: (0,) * nd)


def _fox_in_call(x2, g, wqkv, wfc, bf, qg, kg, cqg, gmat, seq):
    n, _ = x2.shape
    tm = ROW_BLOCK
    pq, pk, oneq, onek = _bias_lane_constants()
    tri = np.tril(np.ones((CUMSUM_BLOCK, CUMSUM_BLOCK), np.float32))
    consts = [jnp.asarray(tri, BF16), jnp.asarray(pq, BF16), jnp.asarray(pk, BF16),
              jnp.asarray(oneq, F32), jnp.asarray(onek, F32)]
    ins = [x2, g, wqkv, wfc, bf, qg, kg, cqg, gmat] + consts
    in_specs = [pl.BlockSpec((tm, D_MODEL), lambda i: (i, 0))]
    in_specs += [_const_spec(a.shape) for a in ins[1:]]
    return pl.pallas_call(
        functools.partial(_fox_in_kernel, blocks_per_seq=seq // tm),
        grid=(n // tm,),
        in_specs=in_specs,
        out_specs=[
            pl.BlockSpec((tm, HEAD_PAIRS * PAIR_AUG), lambda i: (i, 0)),
            pl.BlockSpec((tm, HEAD_PAIRS * PAIR_AUG), lambda i: (i, 0)),
            pl.BlockSpec((tm, TOKEN_WIDTH), lambda i: (i, 0)),
            pl.BlockSpec((tm, CROSS_WIDTH), lambda i: (i, 0)),
        ],
        out_shape=[
            jax.ShapeDtypeStruct((n, HEAD_PAIRS * PAIR_AUG), BF16),
            jax.ShapeDtypeStruct((n, HEAD_PAIRS * PAIR_AUG), BF16),
            jax.ShapeDtypeStruct((n, TOKEN_WIDTH), BF16),
            jax.ShapeDtypeStruct((n, CROSS_WIDTH), BF16),
        ],
        scratch_shapes=[pltpu.VMEM((SUBLANES, LANES), F32)],
        compiler_params=pltpu.CompilerParams(
            dimension_semantics=("arbitrary",), vmem_limit_bytes=VMEM_LIMIT),
        name="fox_in",
    )(*ins)


def _fox_attn_kernel(q_ref, k_ref, v_ref, o_ref, *, tq, tk):
    i = pl.program_id(2)
    q = q_ref[0]
    lane = lax.broadcasted_iota(jnp.int32, (1, PAIR_AUG), 1)
    keep_a = (lane < HEAD_DIM) | ((lane >= LANES) & (lane < LANES + 6))
    keep_b = ((lane >= HEAD_DIM) & (lane < LANES)) | ((lane >= LANES + 6) & (lane < LANES + 12))
    zero = jnp.zeros_like(q)
    q_heads = (jnp.where(keep_a, q, zero), jnp.where(keep_b, q, zero))
    is_a = lax.broadcasted_iota(jnp.int32, (1, LANES), 1) < HEAD_DIM

    def block(j, carry, masked):
        m_a, l_a, m_b, l_b, acc = carry
        start = pl.multiple_of(j * tk, tk)
        ks = k_ref[0, pl.ds(start, tk), :]
        vs = v_ref[0, pl.ds(start, tk), :]
        new = []
        for qh, m_old, l_old in ((q_heads[0], m_a, l_a), (q_heads[1], m_b, l_b)):
            s = _dot_nt(qh, ks)
            if masked:
                row = lax.broadcasted_iota(jnp.int32, (tq, tk), 0)
                col = lax.broadcasted_iota(jnp.int32, (tq, tk), 1)
                s = jnp.where(col <= row, s, NEG)
            m_new = jnp.maximum(m_old, jnp.max(s, axis=-1, keepdims=True))
            alpha = jnp.exp2(m_old - m_new)
            p = jnp.exp2(s - m_new)
            l_new = alpha * l_old + jnp.sum(p, axis=-1, keepdims=True)
            o = _dot(p.astype(BF16), vs)
            new.append((m_new, l_new, alpha, o))
        (m_a, l_a, al_a, o_a), (m_b, l_b, al_b, o_b) = new
        acc = jnp.where(is_a, al_a, al_b) * acc + jnp.where(is_a, o_a, o_b)
        return m_a, l_a, m_b, l_b, acc

    init = (jnp.full((tq, 1), NEG, F32), jnp.zeros((tq, 1), F32),
            jnp.full((tq, 1), NEG, F32), jnp.zeros((tq, 1), F32),
            jnp.zeros((tq, LANES), F32))
    carry = lax.fori_loop(0, i, lambda j, c: block(j, c, False), init)
    _, l_a, _, l_b, acc = block(i, carry, True)
    o_ref[0] = (acc * jnp.where(is_a, 1.0 / l_a, 1.0 / l_b)).astype(BF16)


def _fox_attn_call(qa, ka, v):
    b, s, _ = v.shape
    tq = tk = ATTN_BLOCK
    return pl.pallas_call(
        functools.partial(_fox_attn_kernel, tq=tq, tk=tk),
        grid=(b, HEAD_PAIRS, s // tq),
        in_specs=[
            pl.BlockSpec((1, tq, PAIR_AUG), lambda bi, p, i: (bi, i, p)),
            pl.BlockSpec((1, s, PAIR_AUG), lambda bi, p, i: (bi, 0, p)),
            pl.BlockSpec((1, s, LANES), lambda bi, p, i: (bi, 0, p)),
        ],
        out_specs=pl.BlockSpec((1, tq, LANES), lambda bi, p, i: (bi, i, p)),
        out_shape=jax.ShapeDtypeStruct((b, s, TOKEN_WIDTH), BF16),
        compiler_params=pltpu.CompilerParams(
            dimension_semantics=("arbitrary", "arbitrary", "arbitrary"),
            vmem_limit_bytes=VMEM_LIMIT),
        name="fox_attn",
    )(qa, ka, v)


def _cross_attention(cq, ck, vm):
    is_a = lax.broadcasted_iota(jnp.int32, (1, LANES), 1) < HEAD_DIM
    outs = []
    for p in range(CROSS_WIDTH // LANES):
        sl = slice(p * LANES, (p + 1) * LANES)
        cqp, ckp, vp = cq[:, sl], ck[:, sl], vm[:, sl]
        zero = jnp.zeros_like(cqp)
        halves = []
        for keep in (is_a, jnp.logical_not(is_a)):
            s = _dot_nt(jnp.where(keep, cqp, zero), ckp)
            m = jnp.max(s, axis=-1, keepdims=True)
            pr = jnp.exp2(s - m)
            l = jnp.sum(pr, axis=-1, keepdims=True)
            halves.append(_dot(pr.astype(BF16), vp) * (1.0 / l))
        outs.append(jnp.where(is_a, halves[0], halves[1]))
    return jnp.concatenate(outs, axis=1).astype(BF16)


def _mix_out_fox_kernel(tok_ref, cq_ref, ck_ref, vm_ref, wout_ref, x_ref, o_ref):
    cross = _cross_attention(cq_ref[...], ck_ref[0, 0], vm_ref[0])
    y = _dot(tok_ref[...], wout_ref[0:TOKEN_WIDTH, :]) + _dot(cross, wout_ref[TOKEN_WIDTH:, :])
    o_ref[...] = x_ref[...] + y


def _mix_out_conv_kernel(u_ref, halo_ref, dw_ref, dwb_ref, lng_ref, lnb_ref,
                         cq_ref, ck_ref, vm_ref, wout_ref, x_ref, o_ref, ubuf_ref, *, blocks_per_seq):
    i = pl.program_id(0)
    tm = u_ref.shape[0]
    first = i % blocks_per_seq == 0
    halo = halo_ref[...]
    ubuf_ref[0:CONV_HALO, :] = jnp.where(first, jnp.zeros_like(halo), halo)
    ubuf_ref[CONV_HALO:CONV_HALO + tm, :] = u_ref[...]

    off = CONV_HALO - (CONV_WIDTH - 1)
    cols = []
    for c in range(TOKEN_WIDTH // LANES):
        sl = slice(c * LANES, (c + 1) * LANES)
        acc = jnp.zeros((tm, LANES), F32)
        for k in range(CONV_WIDTH):
            acc = acc + dw_ref[k:k + 1, sl] * ubuf_ref[off + k:off + k + tm, sl]
        cols.append(acc)
    y = jnp.concatenate(cols, axis=1) + dwb_ref[...]

    mu = jnp.mean(y, axis=-1, keepdims=True)
    yc = y - mu
    yn = yc * lax.rsqrt(jnp.mean(yc * yc, axis=-1, keepdims=True) + EPS)
    tok = _silu(yn * lng_ref[...] + lnb_ref[...]).astype(BF16)

    cross = _cross_attention(cq_ref[...], ck_ref[0, 0], vm_ref[0])
    out = _dot(tok, wout_ref[0:TOKEN_WIDTH, :]) + _dot(cross, wout_ref[TOKEN_WIDTH:, :])
    o_ref[...] = x_ref[...] + out


def _mix_out_call(layer, tok_or_u, cq, ck, vm, wout, x2, seq, conv_params=None):
    n, _ = x2.shape
    tm = ROW_BLOCK
    bps = seq // tm
    m = ck.shape[2]
    row = lambda w: pl.BlockSpec((tm, w), lambda i: (i, 0))
    tail_specs = [
        row(CROSS_WIDTH),
        pl.BlockSpec((1, 1, m, CROSS_WIDTH), lambda i: (layer, i // bps, 0, 0)),
        pl.BlockSpec((1, m, CROSS_WIDTH), lambda i: (i // bps, 0, 0)),
        _const_spec(wout.shape),
        row(D_MODEL),
    ]
    params = pltpu.CompilerParams(dimension_semantics=("arbitrary",), vmem_limit_bytes=VMEM_LIMIT)
    out_shape = jax.ShapeDtypeStruct((n, D_MODEL), F32)
    if conv_params is None:
        return pl.pallas_call(
            _mix_out_fox_kernel,
            grid=(n // tm,),
            in_specs=[row(TOKEN_WIDTH)] + tail_specs,
            out_specs=row(D_MODEL),
            out_shape=out_shape,
            compiler_params=params,
            name="mix_out_fox",
        )(tok_or_u, cq, ck, vm, wout, x2)
    dw, dwb, lng, lnb = conv_params
    halo_blocks = tm // CONV_HALO
    halo_spec = pl.BlockSpec((CONV_HALO, TOKEN_WIDTH),
                             lambda i: (jnp.maximum(i * halo_blocks - 1, 0), 0))
    return pl.pallas_call(
        functools.partial(_mix_out_conv_kernel, blocks_per_seq=bps),
        grid=(n // tm,),
        in_specs=[row(TOKEN_WIDTH), halo_spec] + [_const_spec(a.shape) for a in conv_params] + tail_specs,
        out_specs=row(D_MODEL),
        out_shape=out_shape,
        scratch_shapes=[pltpu.VMEM((CONV_HALO + tm, TOKEN_WIDTH), F32)],
        compiler_params=params,
        name="mix_out_conv",
    )(tok_or_u, tok_or_u, dw, dwb, lng, lnb, cq, ck, vm, wout, x2)


def _conv_in_kernel(x_ref, g_ref, w_ref, cqg_ref, gmat_ref, u_ref, cq_ref):
    x = x_ref[...]
    h = (x * _rms_scale(x) * g_ref[...]).astype(BF16)
    a = _dot(h, w_ref[:, 0:TOKEN_WIDTH])
    gate = _dot(h, w_ref[:, TOKEN_WIDTH:2 * TOKEN_WIDTH])
    u_ref[...] = a * jax.nn.sigmoid(gate)
    cq = _dot(h, w_ref[:, 2 * TOKEN_WIDTH:])
    cq_ref[...] = (cq * _head_rms_scale(cq, gmat_ref[...]) * cqg_ref[...]).astype(BF16)


def _conv_in_call(x2, g, w, cqg, gmat):
    n, _ = x2.shape
    tm = ROW_BLOCK
    ins = [x2, g, w, cqg, gmat]
    return pl.pallas_call(
        _conv_in_kernel,
        grid=(n // tm,),
        in_specs=[pl.BlockSpec((tm, D_MODEL), lambda i: (i, 0))] + [_const_spec(a.shape) for a in ins[1:]],
        out_specs=[
            pl.BlockSpec((tm, TOKEN_WIDTH), lambda i: (i, 0)),
            pl.BlockSpec((tm, CROSS_WIDTH), lambda i: (i, 0)),
        ],
        out_shape=[
            jax.ShapeDtypeStruct((n, TOKEN_WIDTH), F32),
            jax.ShapeDtypeStruct((n, CROSS_WIDTH), BF16),
        ],
        compiler_params=pltpu.CompilerParams(
            dimension_semantics=("arbitrary",), vmem_limit_bytes=VMEM_LIMIT),
        name="conv_in",
    )(*ins)


def _ffn_kernel(x_ref, g_ref, wa_ref, wg_ref, ca_ref, cg_ref, wd_ref, o_ref,
                carry_a_ref, carry_g_ref, acc_ref, *, blocks_per_seq, n_chunks):
    i = pl.program_id(0)
    first = i % blocks_per_seq == 0
    x = x_ref[...]
    tm = x.shape[0]
    h = (x * _rms_scale(x) * g_ref[...]).astype(BF16)
    row = lax.broadcasted_iota(jnp.int32, (tm, FFN_CHUNK), 0)
    acc_ref[...] = jnp.zeros_like(acc_ref)

    def causal_conv3(u, w, carry_ref, c):
        tail = carry_ref[c]
        tail = jnp.where(first, jnp.zeros_like(tail), tail)
        carry_ref[c] = u[tm - SUBLANES:, :]
        p1 = pltpu.roll(u, 1, 0)
        p2 = pltpu.roll(u, 2, 0)
        t1 = tail[SUBLANES - 1:SUBLANES, :]
        t2 = tail[SUBLANES - 2:SUBLANES - 1, :]
        p1 = jnp.where(row == 0, t1, p1)
        p2 = jnp.where(row == 0, t2, jnp.where(row == 1, t1, p2))
        return w[0:1, :] * p2 + w[1:2, :] * p1 + w[2:3, :] * u

    def chunk(c, _):
        ua = causal_conv3(_dot(h, wa_ref[c]), ca_ref[c], carry_a_ref, c)
        ug = causal_conv3(_dot(h, wg_ref[c]), cg_ref[c], carry_g_ref, c)
        act = (_silu(ug) * ua).astype(BF16)
        acc_ref[...] += _dot(act, wd_ref[c])
        return 0

    lax.fori_loop(0, n_chunks, chunk, 0)
    o_ref[...] = x + acc_ref[...]


def _ffn_call(x2, g, wa, wg, ca, cg, wd, seq):
    n, _ = x2.shape
    tm = ROW_BLOCK
    n_chunks = wa.shape[0]
    ins = [x2, g, wa, wg, ca, cg, wd]
    return pl.pallas_call(
        functools.partial(_ffn_kernel, blocks_per_seq=seq // tm, n_chunks=n_chunks),
        grid=(n // tm,),
        in_specs=[pl.BlockSpec((tm, D_MODEL), lambda i: (i, 0))] + [_const_spec(a.shape) for a in ins[1:]],
        out_specs=pl.BlockSpec((tm, D_MODEL), lambda i: (i, 0)),
        out_shape=jax.ShapeDtypeStruct((n, D_MODEL), F32),
        scratch_shapes=[
            pltpu.VMEM((n_chunks, SUBLANES, FFN_CHUNK), F32),
            pltpu.VMEM((n_chunks, SUBLANES, FFN_CHUNK), F32),
            pltpu.VMEM((tm, D_MODEL), F32),
        ],
        compiler_params=pltpu.CompilerParams(
            dimension_semantics=("arbitrary",), vmem_limit_bytes=VMEM_LIMIT),
        name="ffn",
    )(*ins)


def _ffn_weights(w_up, conv, w_down):
    n_chunks = FFN_DIM // FFN_CHUNK

    def cols(w):
        return jnp.transpose(w.reshape(w.shape[0], n_chunks, FFN_CHUNK), (1, 0, 2))

    wa = cols(w_up[:, :FFN_DIM].astype(BF16))
    wg = cols(w_up[:, FFN_DIM:].astype(BF16))
    ca = cols(conv[:, :FFN_DIM])
    cg = cols(conv[:, FFN_DIM:])
    wd = w_down.astype(BF16).reshape(n_chunks, FFN_CHUNK, D_MODEL)
    return wa, wg, ca, cg, wd


def _head_tiled(g, heads, scale=1.0):
    return (jnp.tile(g.astype(F32), heads) * scale).reshape(1, heads * HEAD_DIM)


def kernel(x, mem, mem_norm_g, mem_w_kv, mix_norm_g, mix_w_out, cross_q_g, cross_k_g,
           fox_w_in, fox_b_f, fox_q_g, fox_k_g,
           conv_w_in, conv_dw, conv_dw_b, conv_ln_g, conv_ln_b,
           ffn_norm_g, ffn_w_up, ffn_conv, ffn_w_down):
    b, s, d = x.shape
    depth = mix_norm_g.shape[0]
    assert d == D_MODEL and s % ROW_BLOCK == 0 and s % ATTN_BLOCK == 0
    assert ROW_BLOCK % CUMSUM_BLOCK == 0 and ROW_BLOCK % CONV_HALO == 0

    gmat = jnp.asarray(np.kron(np.eye(MXU_DIM // HEAD_DIM), np.ones((HEAD_DIM, HEAD_DIM))), BF16)
    ckg = jnp.stack([_head_tiled(cross_k_g[i], CROSS_HEADS) for i in range(depth)])
    ck, vm = _mem_call(mem, mem_norm_g.reshape(1, d), mem_w_kv.astype(BF16), ckg, gmat)

    x2 = x.reshape(b * s, d)
    for i in range(depth):
        j = i // 2
        g = mix_norm_g[i].reshape(1, d)
        cqg = _head_tiled(cross_q_g[i], CROSS_HEADS, QK_SCALE * LOG2E)
        wout = mix_w_out[i].astype(BF16)
        if i % 2 == 0:
            w = fox_w_in[j]
            wqkv = w[:, :3 * TOKEN_WIDTH].astype(BF16)
            wf = jnp.pad(w[:, 3 * TOKEN_WIDTH:3 * TOKEN_WIDTH + FOX_HEADS], ((0, 0), (0, LANES - FOX_HEADS)))
            wfc = jnp.concatenate([w[:, 3 * TOKEN_WIDTH + FOX_HEADS:], wf], axis=1).astype(BF16)
            bf = jnp.pad(fox_b_f[j].astype(F32), (0, LANES - FOX_HEADS)).reshape(1, LANES)
            qg = _head_tiled(fox_q_g[j], FOX_HEADS, QK_SCALE * LOG2E)
            kg = _head_tiled(fox_k_g[j], FOX_HEADS)
            qa, ka, v, cq = _fox_in_call(x2, g, wqkv, wfc, bf, qg, kg, cqg, gmat, s)
            tok = _fox_attn_call(qa.reshape(b, s, -1), ka.reshape(b, s, -1), v.reshape(b, s, -1))
            x2 = _mix_out_call(i, tok.reshape(b * s, -1), cq, ck, vm, wout, x2, s)
        else:
            u, cq = _conv_in_call(x2, g, conv_w_in[j].astype(BF16), cqg, gmat)
            conv_params = (conv_dw[j].astype(F32), conv_dw_b[j].reshape(1, -1).astype(F32),
                           conv_ln_g[j].reshape(1, -1).astype(F32), conv_ln_b[j].reshape(1, -1).astype(F32))
            x2 = _mix_out_call(i, u, cq, ck, vm, wout, x2, s, conv_params)
        wa, wg, ca, cg, wd = _ffn_weights(ffn_w_up[i], ffn_conv[i], ffn_w_down[i])
        x2 = _ffn_call(x2, ffn_norm_g[i].reshape(1, d), wa, wg, ca, cg, wd, s)
    return x2.reshape(b, s, d)
```

```python
import functools
import math

import numpy as np
import jax
import jax.numpy as jnp
from jax import lax
from jax.experimental import pallas as pl
from jax.experimental.pallas import tpu as pltpu

F32 = jnp.float32
BF16 = jnp.bfloat16

D_MODEL = 1024
HEAD_DIM = 64
CROSS_HEADS = 4
CROSS_WIDTH = CROSS_HEADS * HEAD_DIM
TOKEN_WIDTH = D_MODEL - CROSS_WIDTH
FOX_HEADS = TOKEN_WIDTH // HEAD_DIM
HEAD_PAIRS = FOX_HEADS // 2
CONV_WIDTH = 31
FFN_DIM = 2816
FFN_CONV_WIDTH = 3
EPS = 1e-6
NEG = -1e30
LOG2E = 1.4426950408889634
QK_SCALE = 1.0 / math.sqrt(HEAD_DIM)

LANES = 128
SUBLANES = 8
MXU_DIM = 256
PAIR_AUG = 2 * LANES
VMEM_LIMIT = 56 * 1024 * 1024

ROW_BLOCK = 512
ATTN_BLOCK = 512
FFN_CHUNK = 256
CUMSUM_BLOCK = 256
CONV_HALO = 32


def _dot(a, b):
    return jnp.dot(a, b, preferred_element_type=F32)


def _dot_nt(a, b):
    return lax.dot_general(a, b, (((1,), (1,)), ((), ())), preferred_element_type=F32)


def _split2(x):
    hi = x.astype(BF16)
    lo = (x - hi.astype(F32)).astype(BF16)
    return hi, lo


def _split3(x):
    t1 = x.astype(BF16)
    r1 = x - t1.astype(F32)
    t2 = r1.astype(BF16)
    t3 = (r1 - t2.astype(F32)).astype(BF16)
    return t1, t2, t3


def _rms_scale(x):
    return lax.rsqrt(jnp.mean(x * x, axis=-1, keepdims=True) + EPS)


def _head_rms_scale(x, gmat):
    x2 = x * x
    hi, lo = _split2(x2)
    outs = []
    for c in range(x.shape[1] // MXU_DIM):
        sl = slice(c * MXU_DIM, (c + 1) * MXU_DIM)
        outs.append(_dot(hi[:, sl], gmat) + _dot(lo[:, sl], gmat))
    ss = outs[0] if len(outs) == 1 else jnp.concatenate(outs, axis=1)
    return lax.rsqrt(ss * (1.0 / HEAD_DIM) + EPS)


def _log_sigmoid(z):
    return jnp.minimum(z, 0.0) - jnp.log1p(jnp.exp(-jnp.abs(z)))


def _silu(z):
    return z * jax.nn.sigmoid(z)


def _mem_kernel(mem_ref, g_ref, wkv_ref, ckg_ref, gmat_ref, ck_ref, v_ref):
    x = mem_ref[0]
    h = (x * _rms_scale(x) * g_ref[...]).astype(BF16)
    kv = _dot(h, wkv_ref[...])
    k = kv[:, :CROSS_WIDTH]
    v_ref[0] = kv[:, CROSS_WIDTH:].astype(BF16)
    kn = k * _head_rms_scale(k, gmat_ref[...])
    for i in range(ck_ref.shape[0]):
        ck_ref[i, 0] = (kn * ckg_ref[i]).astype(BF16)


def _mem_call(mem, mem_norm_g, wkv, ckg, gmat):
    b, m, _ = mem.shape
    depth = ckg.shape[0]
    return pl.pallas_call(
        _mem_kernel,
        grid=(b,),
        in_specs=[
            pl.BlockSpec((1, m, D_MODEL), lambda i: (i, 0, 0)),
            pl.BlockSpec((1, D_MODEL), lambda i: (0, 0)),
            pl.BlockSpec((D_MODEL, 2 * CROSS_WIDTH), lambda i: (0, 0)),
            pl.BlockSpec((depth, 1, CROSS_WIDTH), lambda i: (0, 0, 0)),
            pl.BlockSpec((MXU_DIM, MXU_DIM), lambda i: (0, 0)),
        ],
        out_specs=[
            pl.BlockSpec((depth, 1, m, CROSS_WIDTH), lambda i: (0, i, 0, 0)),
            pl.BlockSpec((1, m, CROSS_WIDTH), lambda i: (i, 0, 0)),
        ],
        out_shape=[
            jax.ShapeDtypeStruct((depth, b, m, CROSS_WIDTH), BF16),
            jax.ShapeDtypeStruct((b, m, CROSS_WIDTH), BF16),
        ],
        compiler_params=pltpu.CompilerParams(
            dimension_semantics=("arbitrary",), vmem_limit_bytes=VMEM_LIMIT),
        name="mem_kv",
    )(mem, mem_norm_g, wkv, ckg, gmat)


def _fox_in_kernel(x_ref, g_ref, wqk_ref, wvt_ref, wfc_ref, bf_ref, qg_ref, kg_ref, cqg_ref,
                   gmat_ref, tri_ref, pq_ref, pk_ref, oneq_ref, onek_ref,
                   qa_ref, ka_ref, vt_ref, cq_ref, carry_ref, *, blocks_per_seq):
    i = pl.program_id(0)

    @pl.when(i % blocks_per_seq == 0)
    def _():
        carry_ref[...] = jnp.zeros_like(carry_ref)

    x = x_ref[...]
    tm = x.shape[0]
    h = (x * _rms_scale(x) * g_ref[...]).astype(BF16)
    gmat = gmat_ref[...]

    q = _dot(h, wqk_ref[:, 0:TOKEN_WIDTH])
    qn = (q * _head_rms_scale(q, gmat) * qg_ref[...]).astype(BF16)
    k = _dot(h, wqk_ref[:, TOKEN_WIDTH:2 * TOKEN_WIDTH])
    kn = (k * _head_rms_scale(k, gmat) * kg_ref[...]).astype(BF16)
    vt_ref[0] = _dot_nt(wvt_ref[...], h).astype(BF16)

    fc = _dot(h, wfc_ref[...])
    cq = fc[:, :CROSS_WIDTH]
    cq_ref[...] = (cq * _head_rms_scale(cq, gmat) * cqg_ref[...]).astype(BF16)

    lf = _log_sigmoid(fc[:, CROSS_WIDTH:] + bf_ref[...]) * LOG2E
    tri = tri_ref[...]
    carry = carry_ref[0:1, :]
    cs = []
    for sb in range(tm // CUMSUM_BLOCK):
        t1, t2, t3 = _split3(lf[sb * CUMSUM_BLOCK:(sb + 1) * CUMSUM_BLOCK])
        c_sb = (_dot(tri, t1) + _dot(tri, t2)) + _dot(tri, t3) + carry
        carry = c_sb[CUMSUM_BLOCK - 1:CUMSUM_BLOCK, :]
        cs.append(c_sb)
    carry_ref[0:1, :] = carry
    c = jnp.concatenate(cs, axis=0)
    c3 = jnp.concatenate(_split3(c), axis=1)
    qext = (_dot(c3, pq_ref[...]) + oneq_ref[...]).astype(BF16)
    kext = (_dot(c3, pk_ref[...]) + onek_ref[...]).astype(BF16)

    for p in range(HEAD_PAIRS):
        src = slice(p * LANES, (p + 1) * LANES)
        qa_ref[:, p * PAIR_AUG:p * PAIR_AUG + LANES] = qn[:, src]
        qa_ref[:, p * PAIR_AUG + LANES:(p + 1) * PAIR_AUG] = qext[:, src]
        ka_ref[:, p * PAIR_AUG:p * PAIR_AUG + LANES] = kn[:, src]
        ka_ref[:, p * PAIR_AUG + LANES:(p + 1) * PAIR_AUG] = kext[:, src]


def _bias_lane_constants():
    pq = np.zeros((3 * LANES, TOKEN_WIDTH), np.float32)
    pk = np.zeros((3 * LANES, TOKEN_WIDTH), np.float32)
    oneq = np.zeros((1, TOKEN_WIDTH), np.float32)
    onek = np.zeros((1, TOKEN_WIDTH), np.float32)
    for p in range(HEAD_PAIRS):
        for half in range(2):
            head = 2 * p + half
            base = p * LANES + 6 * half
            for t in range(3):
                pq[t * LANES + head, base + t] = 1.0
                pk[t * LANES + head, base + 3 + t] = -1.0
                oneq[0, base + 3 + t] = 1.0
                onek[0, base + t] = 1.0
    return pq, pk, oneq, onek


def _const_spec(shape):
    nd = len(shape)
    return pl.BlockSpec(shape, lambda i: (0,) * nd)


def _fox_in_call(x2, g, wqk, wvt, wfc, bf, qg, kg, cqg, gmat, seq):
    n, _ = x2.shape
    tm = ROW_BLOCK
    bps = seq // tm
    pq, pk, oneq, onek = _bias_lane_constants()
    tri = np.tril(np.ones((CUMSUM_BLOCK, CUMSUM_BLOCK), np.float32))
    consts = [jnp.asarray(tri, BF16), jnp.asarray(pq, BF16), jnp.asarray(pk, BF16),
              jnp.asarray(oneq, F32), jnp.asarray(onek, F32)]
    ins = [x2, g, wqk, wvt, wfc, bf, qg, kg, cqg, gmat] + consts
    in_specs = [pl.BlockSpec((tm, D_MODEL), lambda i: (i, 0))]
    in_specs += [_const_spec(a.shape) for a in ins[1:]]
    return pl.pallas_call(
        functools.partial(_fox_in_kernel, blocks_per_seq=bps),
        grid=(n // tm,),
        in_specs=in_specs,
        out_specs=[
            pl.BlockSpec((tm, HEAD_PAIRS * PAIR_AUG), lambda i: (i, 0)),
            pl.BlockSpec((tm, HEAD_PAIRS * PAIR_AUG), lambda i: (i, 0)),
            pl.BlockSpec((1, TOKEN_WIDTH, tm), lambda i: (i // bps, 0, i % bps)),
            pl.BlockSpec((tm, CROSS_WIDTH), lambda i: (i, 0)),
        ],
        out_shape=[
            jax.ShapeDtypeStruct((n, HEAD_PAIRS * PAIR_AUG), BF16),
            jax.ShapeDtypeStruct((n, HEAD_PAIRS * PAIR_AUG), BF16),
            jax.ShapeDtypeStruct((n // seq, TOKEN_WIDTH, seq), BF16),
            jax.ShapeDtypeStruct((n, CROSS_WIDTH), BF16),
        ],
        scratch_shapes=[pltpu.VMEM((SUBLANES, LANES), F32)],
        compiler_params=pltpu.CompilerParams(
            dimension_semantics=("arbitrary",), vmem_limit_bytes=VMEM_LIMIT),
        name="fox_in",
    )(*ins)


def _fox_attn_kernel(q_ref, k_ref, vt_ref, o_ref, s_ref, acc_ref, *, tq, tk):
    i = pl.program_id(2)
    q = q_ref[0]
    lane = lax.broadcasted_iota(jnp.int32, (1, PAIR_AUG), 1)
    keep_a = (lane < HEAD_DIM) | ((lane >= LANES) & (lane < LANES + 6))
    keep_b = ((lane >= HEAD_DIM) & (lane < LANES)) | ((lane >= LANES + 6) & (lane < LANES + 12))
    zero = jnp.zeros_like(q)
    qt_heads = tuple(jnp.transpose(jnp.where(keep, q, zero).astype(F32)).astype(BF16)
                     for keep in (keep_a, keep_b))
    acc_ref[...] = jnp.zeros_like(acc_ref)

    def scores(j, slot):
        start = pl.multiple_of(j * tk, tk)
        ks = k_ref[0, pl.ds(start, tk), :]
        for h in range(2):
            s_ref[slot, h] = _dot(ks, qt_heads[h])

    def update(j, slot, stats, mask_offset=None):
        start = pl.multiple_of(j * tk, tk)
        vts = vt_ref[0, :, pl.ds(start, tk)]
        out = []
        for h in range(2):
            m_old, l_old = stats[h]
            s = s_ref[slot, h]
            if mask_offset is not None:
                key = lax.broadcasted_iota(jnp.int32, (tk, tq), 0) + mask_offset
                qry = lax.broadcasted_iota(jnp.int32, (tk, tq), 1)
                s = jnp.where(key <= qry, s, NEG)
            m_new = jnp.maximum(m_old, jnp.max(s, axis=0, keepdims=True))
            alpha = jnp.exp2(m_old - m_new)
            p = jnp.exp2(s - m_new)
            l_new = alpha * l_old + jnp.sum(p, axis=0, keepdims=True)
            rows = slice(h * HEAD_DIM, (h + 1) * HEAD_DIM)
            acc_ref[rows, :] = alpha * acc_ref[rows, :] + _dot(vts[rows, :], p.astype(BF16))
            out.append((m_new, l_new))
        return tuple(out)

    scores(0, 0)

    def body(jj, stats):
        scores(2 * jj + 1, 1)
        stats = update(2 * jj, 0, stats)
        scores(2 * jj + 2, 0)
        return update(2 * jj + 1, 1, stats)

    init = tuple((jnp.full((1, tq), NEG, F32), jnp.zeros((1, tq), F32)) for _ in range(2))
    stats = lax.fori_loop(0, i, body, init)
    scores(2 * i + 1, 1)
    stats = update(2 * i, 0, stats, mask_offset=0)
    stats = update(2 * i + 1, 1, stats, mask_offset=tk)
    inv = jnp.concatenate([jnp.broadcast_to(1.0 / l, (HEAD_DIM, tq)) for _, l in stats], axis=0)
    o_ref[0] = jnp.transpose(acc_ref[...] * inv).astype(BF16)


def _fox_attn_call(qa, ka, vt):
    b, _, s = vt.shape
    tq = ATTN_BLOCK
    tk = tq // 2
    return pl.pallas_call(
        functools.partial(_fox_attn_kernel, tq=tq, tk=tk),
        grid=(b, HEAD_PAIRS, s // tq),
        in_specs=[
            pl.BlockSpec((1, tq, PAIR_AUG), lambda bi, p, i: (bi, i, p)),
            pl.BlockSpec((1, s, PAIR_AUG), lambda bi, p, i: (bi, 0, p)),
            pl.BlockSpec((1, LANES, s), lambda bi, p, i: (bi, p, 0)),
        ],
        out_specs=pl.BlockSpec((1, tq, LANES), lambda bi, p, i: (bi, i, p)),
        out_shape=jax.ShapeDtypeStruct((b, s, TOKEN_WIDTH), BF16),
        scratch_shapes=[
            pltpu.VMEM((2, 2, tk, tq), F32),
            pltpu.VMEM((LANES, tq), F32),
        ],
        compiler_params=pltpu.CompilerParams(
            dimension_semantics=("arbitrary", "arbitrary", "arbitrary"),
            vmem_limit_bytes=VMEM_LIMIT),
        name="fox_attn",
    )(qa, ka, vt)


def _cross_attention(cq, ck, vm):
    is_a = lax.broadcasted_iota(jnp.int32, (1, LANES), 1) < HEAD_DIM
    outs = []
    for p in range(CROSS_WIDTH // LANES):
        sl = slice(p * LANES, (p + 1) * LANES)
        cqp, ckp, vp = cq[:, sl], ck[:, sl], vm[:, sl]
        zero = jnp.zeros_like(cqp)
        halves = []
        for keep in (is_a, jnp.logical_not(is_a)):
            s = _dot_nt(jnp.where(keep, cqp, zero), ckp)
            m = jnp.max(s, axis=-1, keepdims=True)
            pr = jnp.exp2(s - m)
            l = jnp.sum(pr, axis=-1, keepdims=True)
            halves.append(_dot(pr.astype(BF16), vp) * (1.0 / l))
        outs.append(jnp.where(is_a, halves[0], halves[1]))
    return jnp.concatenate(outs, axis=1).astype(BF16)


def _mix_out_fox_kernel(tok_ref, cq_ref, ck_ref, vm_ref, wout_ref, x_ref, o_ref):
    cross = _cross_attention(cq_ref[...], ck_ref[0, 0], vm_ref[0])
    y = _dot(tok_ref[...], wout_ref[0:TOKEN_WIDTH, :]) + _dot(cross, wout_ref[TOKEN_WIDTH:, :])
    o_ref[...] = x_ref[...] + y


def _mix_out_conv_kernel(u_ref, halo_ref, dw_ref, dwb_ref, lng_ref, lnb_ref,
                         cq_ref, ck_ref, vm_ref, wout_ref, x_ref, o_ref, ubuf_ref, *, blocks_per_seq):
    i = pl.program_id(0)
    tm = u_ref.shape[0]
    first = i % blocks_per_seq == 0
    halo = halo_ref[...]
    ubuf_ref[0:CONV_HALO, :] = jnp.where(first, jnp.zeros_like(halo), halo)
    ubuf_ref[CONV_HALO:CONV_HALO + tm, :] = u_ref[...]

    off = CONV_HALO - (CONV_WIDTH - 1)
    cols = []
    for c in range(TOKEN_WIDTH // LANES):
        sl = slice(c * LANES, (c + 1) * LANES)
        acc = jnp.zeros((tm, LANES), F32)
        for k in range(CONV_WIDTH):
            acc = acc + dw_ref[k:k + 1, sl] * ubuf_ref[off + k:off + k + tm, sl]
        cols.append(acc)
    y = jnp.concatenate(cols, axis=1) + dwb_ref[...]

    mu = jnp.mean(y, axis=-1, keepdims=True)
    yc = y - mu
    yn = yc * lax.rsqrt(jnp.mean(yc * yc, axis=-1, keepdims=True) + EPS)
    tok = _silu(yn * lng_ref[...] + lnb_ref[...]).astype(BF16)

    cross = _cross_attention(cq_ref[...], ck_ref[0, 0], vm_ref[0])
    out = _dot(tok, wout_ref[0:TOKEN_WIDTH, :]) + _dot(cross, wout_ref[TOKEN_WIDTH:, :])
    o_ref[...] = x_ref[...] + out


def _mix_out_call(layer, tok_or_u, cq, ck, vm, wout, x2, seq, conv_params=None):
    n, _ = x2.shape
    tm = ROW_BLOCK
    bps = seq // tm
    m = ck.shape[2]
    row = lambda w: pl.BlockSpec((tm, w), lambda i: (i, 0))
    tail_specs = [
        row(CROSS_WIDTH),
        pl.BlockSpec((1, 1, m, CROSS_WIDTH), lambda i: (layer, i // bps, 0, 0)),
        pl.BlockSpec((1, m, CROSS_WIDTH), lambda i: (i // bps, 0, 0)),
        _const_spec(wout.shape),
        row(D_MODEL),
    ]
    params = pltpu.CompilerParams(dimension_semantics=("arbitrary",), vmem_limit_bytes=VMEM_LIMIT)
    out_shape = jax.ShapeDtypeStruct((n, D_MODEL), F32)
    if conv_params is None:
        return pl.pallas_call(
            _mix_out_fox_kernel,
            grid=(n // tm,),
            in_specs=[row(TOKEN_WIDTH)] + tail_specs,
            out_specs=row(D_MODEL),
            out_shape=out_shape,
            compiler_params=params,
            name="mix_out_fox",
        )(tok_or_u, cq, ck, vm, wout, x2)
    dw, dwb, lng, lnb = conv_params
    halo_blocks = tm // CONV_HALO
    halo_spec = pl.BlockSpec((CONV_HALO, TOKEN_WIDTH),
                             lambda i: (jnp.maximum(i * halo_blocks - 1, 0), 0))
    return pl.pallas_call(
        functools.partial(_mix_out_conv_kernel, blocks_per_seq=bps),
        grid=(n // tm,),
        in_specs=[row(TOKEN_WIDTH), halo_spec] + [_const_spec(a.shape) for a in conv_params] + tail_specs,
        out_specs=row(D_MODEL),
        out_shape=out_shape,
        scratch_shapes=[pltpu.VMEM((CONV_HALO + tm, TOKEN_WIDTH), F32)],
        compiler_params=params,
        name="mix_out_conv",
    )(tok_or_u, tok_or_u, dw, dwb, lng, lnb, cq, ck, vm, wout, x2)


def _conv_in_kernel(x_ref, g_ref, w_ref, cqg_ref, gmat_ref, u_ref, cq_ref):
    x = x_ref[...]
    h = (x * _rms_scale(x) * g_ref[...]).astype(BF16)
    a = _dot(h, w_ref[:, 0:TOKEN_WIDTH])
    gate = _dot(h, w_ref[:, TOKEN_WIDTH:2 * TOKEN_WIDTH])
    u_ref[...] = a * jax.nn.sigmoid(gate)
    cq = _dot(h, w_ref[:, 2 * TOKEN_WIDTH:])
    cq_ref[...] = (cq * _head_rms_scale(cq, gmat_ref[...]) * cqg_ref[...]).astype(BF16)


def _conv_in_call(x2, g, w, cqg, gmat):
    n, _ = x2.shape
    tm = ROW_BLOCK
    ins = [x2, g, w, cqg, gmat]
    return pl.pallas_call(
        _conv_in_kernel,
        grid=(n // tm,),
        in_specs=[pl.BlockSpec((tm, D_MODEL), lambda i: (i, 0))] + [_const_spec(a.shape) for a in ins[1:]],
        out_specs=[
            pl.BlockSpec((tm, TOKEN_WIDTH), lambda i: (i, 0)),
            pl.BlockSpec((tm, CROSS_WIDTH), lambda i: (i, 0)),
        ],
        out_shape=[
            jax.ShapeDtypeStruct((n, TOKEN_WIDTH), F32),
            jax.ShapeDtypeStruct((n, CROSS_WIDTH), BF16),
        ],
        compiler_params=pltpu.CompilerParams(
            dimension_semantics=("arbitrary",), vmem_limit_bytes=VMEM_LIMIT),
        name="conv_in",
    )(*ins)


def _ffn_kernel(x_ref, g_ref, wa_ref, wg_ref, ca_ref, cg_ref, wd_ref, o_ref,
                carry_a_ref, carry_g_ref, acc_ref, ubuf_ref, *, blocks_per_seq, n_chunks):
    i = pl.program_id(0)
    first = i % blocks_per_seq == 0
    x = x_ref[...]
    tm = x.shape[0]
    h = (x * _rms_scale(x) * g_ref[...]).astype(BF16)
    acc_ref[...] = jnp.zeros_like(acc_ref)

    def up_project(c):
        slot = c % 2
        for k, (w_ref, carry_ref) in enumerate(((wa_ref, carry_a_ref), (wg_ref, carry_g_ref))):
            u = _dot(h, w_ref[c])
            tail = carry_ref[c]
            ubuf_ref[slot, k, 0:SUBLANES, :] = jnp.where(first, jnp.zeros_like(tail), tail)
            ubuf_ref[slot, k, SUBLANES:SUBLANES + tm, :] = u
            carry_ref[c] = u[tm - SUBLANES:, :]

    def causal_conv3(c, k, w):
        slot = c % 2
        taps = [ubuf_ref[slot, k, SUBLANES - 2 + j:SUBLANES - 2 + j + tm, :] for j in range(FFN_CONV_WIDTH)]
        return w[0:1, :] * taps[0] + w[1:2, :] * taps[1] + w[2:3, :] * taps[2]

    up_project(0)
    for c in range(n_chunks):
        if c + 1 < n_chunks:
            up_project(c + 1)
        ua = causal_conv3(c, 0, ca_ref[c])
        ug = causal_conv3(c, 1, cg_ref[c])
        act = (_silu(ug) * ua).astype(BF16)
        acc_ref[...] += _dot(act, wd_ref[c])
    o_ref[...] = x + acc_ref[...]


def _ffn_call(x2, g, wa, wg, ca, cg, wd, seq):
    n, _ = x2.shape
    tm = ROW_BLOCK
    n_chunks = wa.shape[0]
    ins = [x2, g, wa, wg, ca, cg, wd]
    return pl.pallas_call(
        functools.partial(_ffn_kernel, blocks_per_seq=seq // tm, n_chunks=n_chunks),
        grid=(n // tm,),
        in_specs=[pl.BlockSpec((tm, D_MODEL), lambda i: (i, 0))] + [_const_spec(a.shape) for a in ins[1:]],
        out_specs=pl.BlockSpec((tm, D_MODEL), lambda i: (i, 0)),
        out_shape=jax.ShapeDtypeStruct((n, D_MODEL), F32),
        scratch_shapes=[
            pltpu.VMEM((n_chunks, SUBLANES, FFN_CHUNK), F32),
            pltpu.VMEM((n_chunks, SUBLANES, FFN_CHUNK), F32),
            pltpu.VMEM((tm, D_MODEL), F32),
            pltpu.VMEM((2, 2, SUBLANES + tm, FFN_CHUNK), F32),
        ],
        compiler_params=pltpu.CompilerParams(
            dimension_semantics=("arbitrary",), vmem_limit_bytes=VMEM_LIMIT),
        name="ffn",
    )(*ins)


def _ffn_weights(w_up, conv, w_down):
    n_chunks = FFN_DIM // FFN_CHUNK

    def cols(w):
        return jnp.transpose(w.reshape(w.shape[0], n_chunks, FFN_CHUNK), (1, 0, 2))

    wa = cols(w_up[:, :FFN_DIM].astype(BF16))
    wg = cols(w_up[:, FFN_DIM:].astype(BF16))
    ca = cols(conv[:, :FFN_DIM])
    cg = cols(conv[:, FFN_DIM:])
    wd = w_down.astype(BF16).reshape(n_chunks, FFN_CHUNK, D_MODEL)
    return wa, wg, ca, cg, wd


def _head_tiled(g, heads, scale=1.0):
    return (jnp.tile(g.astype(F32), heads) * scale).reshape(1, heads * HEAD_DIM)


def kernel(x, mem, mem_norm_g, mem_w_kv, mix_norm_g, mix_w_out, cross_q_g, cross_k_g,
           fox_w_in, fox_b_f, fox_q_g, fox_k_g,
           conv_w_in, conv_dw, conv_dw_b, conv_ln_g, conv_ln_b,
           ffn_norm_g, ffn_w_up, ffn_conv, ffn_w_down):
    b, s, d = x.shape
    depth = mix_norm_g.shape[0]
    assert d == D_MODEL and s % ROW_BLOCK == 0 and s % ATTN_BLOCK == 0
    assert ROW_BLOCK % CUMSUM_BLOCK == 0 and ROW_BLOCK % CONV_HALO == 0

    gmat = jnp.asarray(np.kron(np.eye(MXU_DIM // HEAD_DIM), np.ones((HEAD_DIM, HEAD_DIM))), BF16)
    ckg = jnp.stack([_head_tiled(cross_k_g[i], CROSS_HEADS) for i in range(depth)])
    ck, vm = _mem_call(mem, mem_norm_g.reshape(1, d), mem_w_kv.astype(BF16), ckg, gmat)

    x2 = x.reshape(b * s, d)
    for i in range(depth):
        j = i // 2
        g = mix_norm_g[i].reshape(1, d)
        cqg = _head_tiled(cross_q_g[i], CROSS_HEADS, QK_SCALE * LOG2E)
        wout = mix_w_out[i].astype(BF16)
        if i % 2 == 0:
            w = fox_w_in[j]
            wqk = w[:, :2 * TOKEN_WIDTH].astype(BF16)
            wvt = w[:, 2 * TOKEN_WIDTH:3 * TOKEN_WIDTH].T.astype(BF16)
            wf = jnp.pad(w[:, 3 * TOKEN_WIDTH:3 * TOKEN_WIDTH + FOX_HEADS], ((0, 0), (0, LANES - FOX_HEADS)))
            wfc = jnp.concatenate([w[:, 3 * TOKEN_WIDTH + FOX_HEADS:], wf], axis=1).astype(BF16)
            bf = jnp.pad(fox_b_f[j].astype(F32), (0, LANES - FOX_HEADS)).reshape(1, LANES)
            qg = _head_tiled(fox_q_g[j], FOX_HEADS, QK_SCALE * LOG2E)
            kg = _head_tiled(fox_k_g[j], FOX_HEADS)
            qa, ka, vt, cq = _fox_in_call(x2, g, wqk, wvt, wfc, bf, qg, kg, cqg, gmat, s)
            tok = _fox_attn_call(qa.reshape(b, s, -1), ka.reshape(b, s, -1), vt)
            x2 = _mix_out_call(i, tok.reshape(b * s, -1), cq, ck, vm, wout, x2, s)
        else:
            u, cq = _conv_in_call(x2, g, conv_w_in[j].astype(BF16), cqg, gmat)
            conv_params = (conv_dw[j].astype(F32), conv_dw_b[j].reshape(1, -1).astype(F32),
                           conv_ln_g[j].reshape(1, -1).astype(F32), conv_ln_b[j].reshape(1, -1).astype(F32))
            x2 = _mix_out_call(i, u, cq, ck, vm, wout, x2, s, conv_params)
        wa, wg, ca, cg, wd = _ffn_weights(ffn_w_up[i], ffn_conv[i], ffn_w_down[i])
        x2 = _ffn_call(x2, ffn_norm_g[i].reshape(1, d), wa, wg, ca, cg, wd, s)
    return x2.reshape(b, s, d)
```

```python
import functools
import math

import numpy as np
import jax
import jax.numpy as jnp
from jax import lax
from jax.experimental import pallas as pl
from jax.experimental.pallas import tpu as pltpu

F32 = jnp.float32
BF16 = jnp.bfloat16

D_MODEL = 1024
HEAD_DIM = 64
CROSS_HEADS = 4
CROSS_WIDTH = CROSS_HEADS * HEAD_DIM
TOKEN_WIDTH = D_MODEL - CROSS_WIDTH
FOX_HEADS = TOKEN_WIDTH // HEAD_DIM
HEAD_PAIRS = FOX_HEADS // 2
CONV_WIDTH = 31
FFN_DIM = 2816
FFN_CONV_WIDTH = 3
EPS = 1e-6
NEG = -1e30
LOG2E = 1.4426950408889634
QK_SCALE = 1.0 / math.sqrt(HEAD_DIM)

LANES = 128
SUBLANES = 8
MXU_DIM = 256
PAIR_AUG = 2 * LANES
VT_ROWS = HEAD_DIM + 16
VMEM_LIMIT = 56 * 1024 * 1024

ROW_BLOCK = 512
ATTN_BLOCK = 1024
ATTN_KV_BLOCK = 256
FFN_CHUNK = 256
CUMSUM_BLOCK = 256
CONV_HALO = 32
CONV_ROWS = 128


def _dot(a, b):
    return jnp.dot(a, b, preferred_element_type=F32)


def _dot_nt(a, b):
    return lax.dot_general(a, b, (((1,), (1,)), ((), ())), preferred_element_type=F32)


def _split2(x):
    hi = x.astype(BF16)
    lo = (x - hi.astype(F32)).astype(BF16)
    return hi, lo


def _split3(x):
    t1 = x.astype(BF16)
    r1 = x - t1.astype(F32)
    t2 = r1.astype(BF16)
    t3 = (r1 - t2.astype(F32)).astype(BF16)
    return t1, t2, t3


def _rms_scale(x):
    return lax.rsqrt(jnp.mean(x * x, axis=-1, keepdims=True) + EPS)


def _head_rms_scale(x, gmat):
    x2 = x * x
    hi, lo = _split2(x2)
    outs = []
    for c in range(x.shape[1] // MXU_DIM):
        sl = slice(c * MXU_DIM, (c + 1) * MXU_DIM)
        outs.append(_dot(hi[:, sl], gmat) + _dot(lo[:, sl], gmat))
    ss = outs[0] if len(outs) == 1 else jnp.concatenate(outs, axis=1)
    return lax.rsqrt(ss * (1.0 / HEAD_DIM) + EPS)


def _log_sigmoid(z):
    return jnp.minimum(z, 0.0) - jnp.log1p(jnp.exp(-jnp.abs(z)))


def _silu(z):
    return z * jax.nn.sigmoid(z)


def _mem_kernel(mem_ref, g_ref, wkv_ref, ckg_ref, gmat_ref, ck_ref, v_ref):
    x = mem_ref[0]
    h = (x * _rms_scale(x) * g_ref[...]).astype(BF16)
    kv = _dot(h, wkv_ref[...])
    k = kv[:, :CROSS_WIDTH]
    v_ref[0] = kv[:, CROSS_WIDTH:].astype(BF16)
    kn = k * _head_rms_scale(k, gmat_ref[...])
    for i in range(ck_ref.shape[0]):
        ck_ref[i, 0] = (kn * ckg_ref[i]).astype(BF16)


def _mem_call(mem, mem_norm_g, wkv, ckg, gmat):
    b, m, _ = mem.shape
    depth = ckg.shape[0]
    return pl.pallas_call(
        _mem_kernel,
        grid=(b,),
        in_specs=[
            pl.BlockSpec((1, m, D_MODEL), lambda i: (i, 0, 0)),
            pl.BlockSpec((1, D_MODEL), lambda i: (0, 0)),
            pl.BlockSpec((D_MODEL, 2 * CROSS_WIDTH), lambda i: (0, 0)),
            pl.BlockSpec((depth, 1, CROSS_WIDTH), lambda i: (0, 0, 0)),
            pl.BlockSpec((MXU_DIM, MXU_DIM), lambda i: (0, 0)),
        ],
        out_specs=[
            pl.BlockSpec((depth, 1, m, CROSS_WIDTH), lambda i: (0, i, 0, 0)),
            pl.BlockSpec((1, m, CROSS_WIDTH), lambda i: (i, 0, 0)),
        ],
        out_shape=[
            jax.ShapeDtypeStruct((depth, b, m, CROSS_WIDTH), BF16),
            jax.ShapeDtypeStruct((b, m, CROSS_WIDTH), BF16),
        ],
        compiler_params=pltpu.CompilerParams(
            dimension_semantics=("arbitrary",), vmem_limit_bytes=VMEM_LIMIT),
        name="mem_kv",
    )(mem, mem_norm_g, wkv, ckg, gmat)


def _fox_in_kernel(x_ref, g_ref, wqk_ref, wvt_ref, wfc_ref, bf_ref, qg_ref, kg_ref, cqg_ref,
                   gmat_ref, tri_ref, pq_ref, pk_ref, oneq_ref, onek_ref,
                   qa_ref, ka_ref, vt_ref, cq_ref, carry_ref, *, blocks_per_seq):
    i = pl.program_id(0)

    @pl.when(i % blocks_per_seq == 0)
    def _():
        carry_ref[...] = jnp.zeros_like(carry_ref)

    x = x_ref[...]
    tm = x.shape[0]
    h = (x * _rms_scale(x) * g_ref[...]).astype(BF16)
    gmat = gmat_ref[...]

    q = _dot(h, wqk_ref[:, 0:TOKEN_WIDTH])
    qn = (q * _head_rms_scale(q, gmat) * qg_ref[...]).astype(BF16)
    k = _dot(h, wqk_ref[:, TOKEN_WIDTH:2 * TOKEN_WIDTH])
    kn = (k * _head_rms_scale(k, gmat) * kg_ref[...]).astype(BF16)
    vt = _dot_nt(wvt_ref[...], h).astype(BF16)
    ones = jnp.ones((VT_ROWS - HEAD_DIM, tm), BF16)
    for hd in range(FOX_HEADS):
        vt_ref[0, hd * VT_ROWS:hd * VT_ROWS + HEAD_DIM, :] = vt[hd * HEAD_DIM:(hd + 1) * HEAD_DIM, :]
        vt_ref[0, hd * VT_ROWS + HEAD_DIM:(hd + 1) * VT_ROWS, :] = ones

    fc = _dot(h, wfc_ref[...])
    cq = fc[:, :CROSS_WIDTH]
    cq_ref[...] = (cq * _head_rms_scale(cq, gmat) * cqg_ref[...]).astype(BF16)

    lf = _log_sigmoid(fc[:, CROSS_WIDTH:] + bf_ref[...]) * LOG2E
    tri = tri_ref[...]
    carry = carry_ref[0:1, :]
    cs = []
    for sb in range(tm // CUMSUM_BLOCK):
        t1, t2, t3 = _split3(lf[sb * CUMSUM_BLOCK:(sb + 1) * CUMSUM_BLOCK])
        c_sb = (_dot(tri, t1) + _dot(tri, t2)) + _dot(tri, t3) + carry
        carry = c_sb[CUMSUM_BLOCK - 1:CUMSUM_BLOCK, :]
        cs.append(c_sb)
    carry_ref[0:1, :] = carry
    c = jnp.concatenate(cs, axis=0)
    t1, t2, t3 = _split3(c)
    grp = lax.broadcasted_iota(jnp.int32, (1, LANES), 1) // FOX_HEADS
    c3 = jnp.where(grp == 0, t1, jnp.where(grp == 1, t2, t3))
    qext = (_dot(c3, pq_ref[...]) + oneq_ref[...]).astype(BF16)
    kext = (_dot(c3, pk_ref[...]) + onek_ref[...]).astype(BF16)

    for p in range(HEAD_PAIRS):
        src = slice(p * LANES, (p + 1) * LANES)
        qa_ref[:, p * PAIR_AUG:p * PAIR_AUG + LANES] = qn[:, src]
        qa_ref[:, p * PAIR_AUG + LANES:(p + 1) * PAIR_AUG] = qext[:, src]
        ka_ref[:, p * PAIR_AUG:p * PAIR_AUG + LANES] = kn[:, src]
        ka_ref[:, p * PAIR_AUG + LANES:(p + 1) * PAIR_AUG] = kext[:, src]


def _bias_lane_constants():
    pq = np.zeros((LANES, TOKEN_WIDTH), np.float32)
    pk = np.zeros((LANES, TOKEN_WIDTH), np.float32)
    oneq = np.zeros((1, TOKEN_WIDTH), np.float32)
    onek = np.zeros((1, TOKEN_WIDTH), np.float32)
    for p in range(HEAD_PAIRS):
        for half in range(2):
            head = 2 * p + half
            base = p * LANES + 6 * half
            for t in range(3):
                pq[t * FOX_HEADS + head, base + t] = 1.0
                pk[t * FOX_HEADS + head, base + 3 + t] = -1.0
                oneq[0, base + 3 + t] = 1.0
                onek[0, base + t] = 1.0
    return pq, pk, oneq, onek


def _const_spec(shape):
    nd = len(shape)
    return pl.BlockSpec(shape, lambda i: (0,) * nd)


def _fox_in_call(x2, g, wqk, wvt, wfc, bf, qg, kg, cqg, gmat, seq):
    n, _ = x2.shape
    tm = ROW_BLOCK
    bps = seq // tm
    pq, pk, oneq, onek = _bias_lane_constants()
    tri = np.tril(np.ones((CUMSUM_BLOCK, CUMSUM_BLOCK), np.float32))
    consts = [jnp.asarray(tri, BF16), jnp.asarray(pq, BF16), jnp.asarray(pk, BF16),
              jnp.asarray(oneq, F32), jnp.asarray(onek, F32)]
    ins = [x2, g, wqk, wvt, wfc, bf, qg, kg, cqg, gmat] + consts
    in_specs = [pl.BlockSpec((tm, D_MODEL), lambda i: (i, 0))]
    in_specs += [_const_spec(a.shape) for a in ins[1:]]
    return pl.pallas_call(
        functools.partial(_fox_in_kernel, blocks_per_seq=bps),
        grid=(n // tm,),
        in_specs=in_specs,
        out_specs=[
            pl.BlockSpec((tm, HEAD_PAIRS * PAIR_AUG), lambda i: (i, 0)),
            pl.BlockSpec((tm, HEAD_PAIRS * PAIR_AUG), lambda i: (i, 0)),
            pl.BlockSpec((1, FOX_HEADS * VT_ROWS, tm), lambda i: (i // bps, 0, i % bps)),
            pl.BlockSpec((tm, CROSS_WIDTH), lambda i: (i, 0)),
        ],
        out_shape=[
            jax.ShapeDtypeStruct((n, HEAD_PAIRS * PAIR_AUG), BF16),
            jax.ShapeDtypeStruct((n, HEAD_PAIRS * PAIR_AUG), BF16),
            jax.ShapeDtypeStruct((n // seq, FOX_HEADS * VT_ROWS, seq), BF16),
            jax.ShapeDtypeStruct((n, CROSS_WIDTH), BF16),
        ],
        scratch_shapes=[pltpu.VMEM((SUBLANES, LANES), F32)],
        compiler_params=pltpu.CompilerParams(
            dimension_semantics=("arbitrary",), vmem_limit_bytes=VMEM_LIMIT),
        name="fox_in",
    )(*ins)


def _fox_attn_kernel(q_ref, k_ref, vt_ref, o_ref, s_ref, mx_ref, acc_ref, *, tq, tk):
    lane = lax.broadcasted_iota(jnp.int32, (1, PAIR_AUG), 1)
    keep_a = (lane < HEAD_DIM) | ((lane >= LANES) & (lane < LANES + 6))
    keep_b = ((lane >= HEAD_DIM) & (lane < LANES)) | ((lane >= LANES + 6) & (lane < LANES + 12))
    nsub = tq // tk

    def q_block(i):
        q = q_ref[0, i * tq:(i + 1) * tq, :]
        zero = jnp.zeros_like(q)
        qt_heads = tuple(jnp.transpose(jnp.where(keep, q, zero).astype(F32)).astype(BF16)
                         for keep in (keep_a, keep_b))
        acc_ref[...] = jnp.zeros_like(acc_ref)

        def scores(j, slot, c0=0, want_max=True):
            start = j * tk if isinstance(j, int) else pl.multiple_of(j * tk, tk)
            ks = k_ref[0, pl.ds(start, tk), :]
            for h in range(2):
                s = _dot(ks, qt_heads[h][:, c0:])
                s_ref[slot, h, :, c0:] = s
                if want_max:
                    mx_ref[slot, h, 0:1, :] = jnp.max(s, axis=0, keepdims=True)

        def update(j, slot, ms, c0=0, masked=False):
            start = j * tk if isinstance(j, int) else pl.multiple_of(j * tk, tk)
            vts = vt_ref[0, :, pl.ds(start, tk)]
            out = []
            for h in range(2):
                m_old = ms[h]
                s = s_ref[slot, h, :, c0:]
                if masked:
                    key = lax.broadcasted_iota(jnp.int32, s.shape, 0)
                    qry = lax.broadcasted_iota(jnp.int32, s.shape, 1)
                    s = jnp.where(key <= qry, s, NEG)
                    s_max = jnp.max(s, axis=0, keepdims=True)
                else:
                    s_max = mx_ref[slot, h, 0:1, :]
                m_new = jnp.maximum(m_old, s_max)
                alpha = jnp.exp2(m_old - m_new)
                p = jnp.exp2(s - m_new).astype(BF16)
                rows = slice(h * VT_ROWS, (h + 1) * VT_ROWS)
                acc_ref[rows, c0:] = alpha * acc_ref[rows, c0:] + _dot(vts[rows, :], p)
                out.append(m_new)
            return tuple(out)

        scores(0, 0)

        def trip(t, ms):
            for u in range(nsub):
                blk = nsub * t + u
                scores(blk + 1, (u + 1) % 2)
                ms = update(blk, u % 2, ms)
            return ms

        ms = tuple(jnp.full((1, tq), NEG, F32) for _ in range(2))
        if i > 0:
            ms = lax.fori_loop(0, i, trip, ms)
        for d in range(nsub):
            if d + 1 < nsub:
                scores(nsub * i + d + 1, (d + 1) % 2, c0=(d + 1) * tk, want_max=False)
            ms = update(nsub * i + d, d % 2, ms, c0=d * tk, masked=True)
            if d + 1 < nsub:
                ms = tuple(m[:, tk:] for m in ms)
        outs = []
        for h in range(2):
            blk = acc_ref[h * VT_ROWS:(h + 1) * VT_ROWS, :]
            outs.append(blk[:HEAD_DIM, :] * (1.0 / blk[HEAD_DIM:HEAD_DIM + 1, :]))
        o_ref[0, i * tq:(i + 1) * tq, :] = jnp.transpose(jnp.concatenate(outs, axis=0)).astype(BF16)

    for i in range(q_ref.shape[1] // tq):
        q_block(i)


def _fox_attn_call(qa, ka, vt):
    b, _, s = vt.shape
    tq = ATTN_BLOCK
    tk = ATTN_KV_BLOCK
    return pl.pallas_call(
        functools.partial(_fox_attn_kernel, tq=tq, tk=tk),
        grid=(b, HEAD_PAIRS),
        in_specs=[
            pl.BlockSpec((1, s, PAIR_AUG), lambda bi, p: (bi, 0, p)),
            pl.BlockSpec((1, s, PAIR_AUG), lambda bi, p: (bi, 0, p)),
            pl.BlockSpec((1, 2 * VT_ROWS, s), lambda bi, p: (bi, p, 0)),
        ],
        out_specs=pl.BlockSpec((1, s, LANES), lambda bi, p: (bi, 0, p)),
        out_shape=jax.ShapeDtypeStruct((b, s, TOKEN_WIDTH), BF16),
        scratch_shapes=[
            pltpu.VMEM((2, 2, tk, tq), F32),
            pltpu.VMEM((2, 2, SUBLANES, tq), F32),
            pltpu.VMEM((2 * VT_ROWS, tq), F32),
        ],
        compiler_params=pltpu.CompilerParams(
            dimension_semantics=("arbitrary", "arbitrary"),
            vmem_limit_bytes=VMEM_LIMIT),
        name="fox_attn",
    )(qa, ka, vt)


def _cross_attention(cq, ck, vm):
    is_a = lax.broadcasted_iota(jnp.int32, (1, LANES), 1) < HEAD_DIM
    outs = []
    for p in range(CROSS_WIDTH // LANES):
        sl = slice(p * LANES, (p + 1) * LANES)
        cqp, ckp, vp = cq[:, sl], ck[:, sl], vm[:, sl]
        zero = jnp.zeros_like(cqp)
        halves = []
        for keep in (is_a, jnp.logical_not(is_a)):
            s = _dot_nt(jnp.where(keep, cqp, zero), ckp)
            m = jnp.max(s, axis=-1, keepdims=True)
            pr = jnp.exp2(s - m)
            l = jnp.sum(pr, axis=-1, keepdims=True)
            halves.append(_dot(pr.astype(BF16), vp) * (1.0 / l))
        outs.append(jnp.where(is_a, halves[0], halves[1]))
    return jnp.concatenate(outs, axis=1).astype(BF16)


def _mix_out_fox_kernel(tok_ref, cq_ref, ck_ref, vm_ref, wout_ref, x_ref, o_ref):
    cross = _cross_attention(cq_ref[...], ck_ref[0, 0], vm_ref[0])
    y = _dot(tok_ref[...], wout_ref[0:TOKEN_WIDTH, :]) + _dot(cross, wout_ref[TOKEN_WIDTH:, :])
    o_ref[...] = x_ref[...] + y


def _mix_out_conv_kernel(u_ref, halo_ref, dw_ref, dwb_ref, lng_ref, lnb_ref,
                         cq_ref, ck_ref, vm_ref, wout_ref, x_ref, o_ref, ubuf_ref, y_ref, *, blocks_per_seq):
    i = pl.program_id(0)
    tm = u_ref.shape[0]
    first = i % blocks_per_seq == 0
    halo = halo_ref[...]
    ubuf_ref[0:CONV_HALO, :] = jnp.where(first, jnp.zeros_like(halo), halo)
    ubuf_ref[CONV_HALO:CONV_HALO + tm, :] = u_ref[...]

    off = CONV_HALO - (CONV_WIDTH - 1)

    def conv_rows(rb, _):
        r0 = pl.multiple_of(rb * CONV_ROWS, CONV_ROWS)
        for c in range(TOKEN_WIDTH // LANES):
            sl = slice(c * LANES, (c + 1) * LANES)
            y_blk = None
            for res in range(SUBLANES):
                ext = CONV_ROWS + (SUBLANES if res else 0)
                z = None
                for k in range(CONV_WIDTH):
                    if (off + k) % SUBLANES != res:
                        continue
                    term = dw_ref[k:k + 1, sl] * ubuf_ref[pl.ds(r0 + (off + k - res), ext), sl]
                    z = term if z is None else z + term
                part = z[res:res + CONV_ROWS, :]
                y_blk = part if y_blk is None else y_blk + part
            y_ref[pl.ds(r0, CONV_ROWS), sl] = y_blk + dwb_ref[:, sl]
        return 0

    lax.fori_loop(0, tm // CONV_ROWS, conv_rows, 0)
    y = y_ref[...]

    mu = jnp.mean(y, axis=-1, keepdims=True)
    yc = y - mu
    yn = yc * lax.rsqrt(jnp.mean(yc * yc, axis=-1, keepdims=True) + EPS)
    tok = _silu(yn * lng_ref[...] + lnb_ref[...]).astype(BF16)

    cross = _cross_attention(cq_ref[...], ck_ref[0, 0], vm_ref[0])
    out = _dot(tok, wout_ref[0:TOKEN_WIDTH, :]) + _dot(cross, wout_ref[TOKEN_WIDTH:, :])
    o_ref[...] = x_ref[...] + out


def _mix_out_call(layer, tok_or_u, cq, ck, vm, wout, x2, seq, conv_params=None):
    n, _ = x2.shape
    tm = ROW_BLOCK
    bps = seq // tm
    m = ck.shape[2]
    row = lambda w: pl.BlockSpec((tm, w), lambda i: (i, 0))
    tail_specs = [
        row(CROSS_WIDTH),
        pl.BlockSpec((1, 1, m, CROSS_WIDTH), lambda i: (layer, i // bps, 0, 0)),
        pl.BlockSpec((1, m, CROSS_WIDTH), lambda i: (i // bps, 0, 0)),
        _const_spec(wout.shape),
        row(D_MODEL),
    ]
    params = pltpu.CompilerParams(dimension_semantics=("arbitrary",), vmem_limit_bytes=VMEM_LIMIT)
    out_shape = jax.ShapeDtypeStruct((n, D_MODEL), F32)
    if conv_params is None:
        return pl.pallas_call(
            _mix_out_fox_kernel,
            grid=(n // tm,),
            in_specs=[row(TOKEN_WIDTH)] + tail_specs,
            out_specs=row(D_MODEL),
            out_shape=out_shape,
            compiler_params=params,
            name="mix_out_fox",
        )(tok_or_u, cq, ck, vm, wout, x2)
    dw, dwb, lng, lnb = conv_params
    halo_blocks = tm // CONV_HALO
    halo_spec = pl.BlockSpec((CONV_HALO, TOKEN_WIDTH),
                             lambda i: (jnp.maximum(i * halo_blocks - 1, 0), 0))
    return pl.pallas_call(
        functools.partial(_mix_out_conv_kernel, blocks_per_seq=bps),
        grid=(n // tm,),
        in_specs=[row(TOKEN_WIDTH), halo_spec] + [_const_spec(a.shape) for a in conv_params] + tail_specs,
        out_specs=row(D_MODEL),
        out_shape=out_shape,
        scratch_shapes=[pltpu.VMEM((CONV_HALO + tm, TOKEN_WIDTH), F32),
                        pltpu.VMEM((tm, TOKEN_WIDTH), F32)],
        compiler_params=params,
        name="mix_out_conv",
    )(tok_or_u, tok_or_u, dw, dwb, lng, lnb, cq, ck, vm, wout, x2)


def _conv_in_kernel(x_ref, g_ref, w_ref, cqg_ref, gmat_ref, u_ref, cq_ref):
    x = x_ref[...]
    h = (x * _rms_scale(x) * g_ref[...]).astype(BF16)
    a = _dot(h, w_ref[:, 0:TOKEN_WIDTH])
    gate = _dot(h, w_ref[:, TOKEN_WIDTH:2 * TOKEN_WIDTH])
    u_ref[...] = a * jax.nn.sigmoid(gate)
    cq = _dot(h, w_ref[:, 2 * TOKEN_WIDTH:])
    cq_ref[...] = (cq * _head_rms_scale(cq, gmat_ref[...]) * cqg_ref[...]).astype(BF16)


def _conv_in_call(x2, g, w, cqg, gmat):
    n, _ = x2.shape
    tm = ROW_BLOCK
    ins = [x2, g, w, cqg, gmat]
    return pl.pallas_call(
        _conv_in_kernel,
        grid=(n // tm,),
        in_specs=[pl.BlockSpec((tm, D_MODEL), lambda i: (i, 0))] + [_const_spec(a.shape) for a in ins[1:]],
        out_specs=[
            pl.BlockSpec((tm, TOKEN_WIDTH), lambda i: (i, 0)),
            pl.BlockSpec((tm, CROSS_WIDTH), lambda i: (i, 0)),
        ],
        out_shape=[
            jax.ShapeDtypeStruct((n, TOKEN_WIDTH), F32),
            jax.ShapeDtypeStruct((n, CROSS_WIDTH), BF16),
        ],
        compiler_params=pltpu.CompilerParams(
            dimension_semantics=("arbitrary",), vmem_limit_bytes=VMEM_LIMIT),
        name="conv_in",
    )(*ins)


def _ffn_kernel(x_ref, g_ref, wup_ref, conv_ref, wd_ref, o_ref,
                carry_ref, acc_ref, ubuf_ref, *, blocks_per_seq):
    i = pl.program_id(0)
    first = i % blocks_per_seq == 0
    x = x_ref[...]
    tm = x.shape[0]
    n_chunks = FFN_DIM // FFN_CHUNK
    h = (x * _rms_scale(x) * g_ref[...]).astype(BF16)
    acc_ref[...] = jnp.zeros_like(acc_ref)

    def cols(c, k):
        return slice(k * FFN_DIM + c * FFN_CHUNK, k * FFN_DIM + (c + 1) * FFN_CHUNK)

    def up_project(c):
        slot = c % 2
        for k in range(2):
            u = _dot(h, wup_ref[:, cols(c, k)])
            tail = carry_ref[k, c]
            ubuf_ref[slot, k, 0:SUBLANES, :] = jnp.where(first, jnp.zeros_like(tail), tail)
            ubuf_ref[slot, k, SUBLANES:SUBLANES + tm, :] = u
            carry_ref[k, c] = u[tm - SUBLANES:, :]

    def causal_conv3(c, k):
        slot = c % 2
        w = conv_ref[:, cols(c, k)]
        taps = [ubuf_ref[slot, k, SUBLANES - 2 + j:SUBLANES - 2 + j + tm, :] for j in range(FFN_CONV_WIDTH)]
        return w[0:1, :] * taps[0] + w[1:2, :] * taps[1] + w[2:3, :] * taps[2]

    up_project(0)
    for c in range(n_chunks):
        if c + 1 < n_chunks:
            up_project(c + 1)
        act = (_silu(causal_conv3(c, 1)) * causal_conv3(c, 0)).astype(BF16)
        acc_ref[...] += _dot(act, wd_ref[c * FFN_CHUNK:(c + 1) * FFN_CHUNK, :])
    o_ref[...] = x + acc_ref[...]


def _ffn_call(x2, g, wup, conv, wd, seq):
    n, _ = x2.shape
    tm = ROW_BLOCK
    n_chunks = FFN_DIM // FFN_CHUNK
    ins = [x2, g, wup, conv, wd]
    return pl.pallas_call(
        functools.partial(_ffn_kernel, blocks_per_seq=seq // tm),
        grid=(n // tm,),
        in_specs=[pl.BlockSpec((tm, D_MODEL), lambda i: (i, 0))] + [_const_spec(a.shape) for a in ins[1:]],
        out_specs=pl.BlockSpec((tm, D_MODEL), lambda i: (i, 0)),
        out_shape=jax.ShapeDtypeStruct((n, D_MODEL), F32),
        scratch_shapes=[
            pltpu.VMEM((2, n_chunks, SUBLANES, FFN_CHUNK), F32),
            pltpu.VMEM((tm, D_MODEL), F32),
            pltpu.VMEM((2, 2, SUBLANES + tm, FFN_CHUNK), F32),
        ],
        compiler_params=pltpu.CompilerParams(
            dimension_semantics=("arbitrary",), vmem_limit_bytes=VMEM_LIMIT),
        name="ffn",
    )(*ins)


def _head_tiled(g, heads, scale=1.0):
    return (jnp.tile(g.astype(F32), heads) * scale).reshape(1, heads * HEAD_DIM)


def kernel(x, mem, mem_norm_g, mem_w_kv, mix_norm_g, mix_w_out, cross_q_g, cross_k_g,
           fox_w_in, fox_b_f, fox_q_g, fox_k_g,
           conv_w_in, conv_dw, conv_dw_b, conv_ln_g, conv_ln_b,
           ffn_norm_g, ffn_w_up, ffn_conv, ffn_w_down):
    b, s, d = x.shape
    depth = mix_norm_g.shape[0]
    assert d == D_MODEL and s % ROW_BLOCK == 0 and s % ATTN_BLOCK == 0
    assert ATTN_BLOCK % (2 * ATTN_KV_BLOCK) == 0
    assert ROW_BLOCK % CUMSUM_BLOCK == 0 and ROW_BLOCK % CONV_HALO == 0 and ROW_BLOCK % CONV_ROWS == 0

    gmat = jnp.asarray(np.kron(np.eye(MXU_DIM // HEAD_DIM), np.ones((HEAD_DIM, HEAD_DIM))), BF16)
    ckg = jnp.stack([_head_tiled(cross_k_g[i], CROSS_HEADS) for i in range(depth)])
    ck, vm = _mem_call(mem, mem_norm_g.reshape(1, d), mem_w_kv.astype(BF16), ckg, gmat)

    x2 = x.reshape(b * s, d)
    for i in range(depth):
        j = i // 2
        g = mix_norm_g[i].reshape(1, d)
        cqg = _head_tiled(cross_q_g[i], CROSS_HEADS, QK_SCALE * LOG2E)
        wout = mix_w_out[i].astype(BF16)
        if i % 2 == 0:
            w = fox_w_in[j]
            wqk = w[:, :2 * TOKEN_WIDTH].astype(BF16)
            wvt = w[:, 2 * TOKEN_WIDTH:3 * TOKEN_WIDTH].T.astype(BF16)
            wf = jnp.tile(w[:, 3 * TOKEN_WIDTH:3 * TOKEN_WIDTH + FOX_HEADS], (1, 3))
            wf = jnp.pad(wf, ((0, 0), (0, LANES - 3 * FOX_HEADS)))
            wfc = jnp.concatenate([w[:, 3 * TOKEN_WIDTH + FOX_HEADS:], wf], axis=1).astype(BF16)
            bf = jnp.pad(jnp.tile(fox_b_f[j].astype(F32), 3), (0, LANES - 3 * FOX_HEADS)).reshape(1, LANES)
            qg = _head_tiled(fox_q_g[j], FOX_HEADS, QK_SCALE * LOG2E)
            kg = _head_tiled(fox_k_g[j], FOX_HEADS)
            qa, ka, vt, cq = _fox_in_call(x2, g, wqk, wvt, wfc, bf, qg, kg, cqg, gmat, s)
            tok = _fox_attn_call(qa.reshape(b, s, -1), ka.reshape(b, s, -1), vt)
            x2 = _mix_out_call(i, tok.reshape(b * s, -1), cq, ck, vm, wout, x2, s)
        else:
            u, cq = _conv_in_call(x2, g, conv_w_in[j].astype(BF16), cqg, gmat)
            conv_params = (conv_dw[j].astype(F32), conv_dw_b[j].reshape(1, -1).astype(F32),
                           conv_ln_g[j].reshape(1, -1).astype(F32), conv_ln_b[j].reshape(1, -1).astype(F32))
            x2 = _mix_out_call(i, u, cq, ck, vm, wout, x2, s, conv_params)
        x2 = _ffn_call(x2, ffn_norm_g[i].reshape(1, d), ffn_w_up[i].astype(BF16),
                       ffn_conv[i].astype(F32), ffn_w_down[i].astype(BF16), s)
    return x2.reshape(b, s, d)
```

```python
import functools
import math

import numpy as np
import jax
import jax.numpy as jnp
from jax import lax
from jax.experimental import pallas as pl
from jax.experimental.pallas import tpu as pltpu

F32 = jnp.float32
BF16 = jnp.bfloat16

D_MODEL = 1024
HEAD_DIM = 64
CROSS_HEADS = 4
CROSS_WIDTH = CROSS_HEADS * HEAD_DIM
TOKEN_WIDTH = D_MODEL - CROSS_WIDTH
FOX_HEADS = TOKEN_WIDTH // HEAD_DIM
HEAD_PAIRS = FOX_HEADS // 2
CONV_WIDTH = 31
FFN_DIM = 2816
FFN_CONV_WIDTH = 3
EPS = 1e-6
NEG = -1e30
LOG2E = 1.4426950408889634
QK_SCALE = 1.0 / math.sqrt(HEAD_DIM)

LANES = 128
SUBLANES = 8
MXU_DIM = 256
PAIR_AUG = 2 * LANES
VT_ROWS = HEAD_DIM + 16
VMEM_LIMIT = 56 * 1024 * 1024

ROW_BLOCK = 512
FFN_ROW_BLOCK = 512
ATTN_BLOCK = 1024
ATTN_KV_BLOCK = 256
FFN_CHUNK = 256
CUMSUM_BLOCK = 256
CONV_HALO = 32
CONV_ROWS = 128


def _dot(a, b):
    return jnp.dot(a, b, preferred_element_type=F32)


def _dot_nt(a, b):
    return lax.dot_general(a, b, (((1,), (1,)), ((), ())), preferred_element_type=F32)


def _split2(x):
    hi = x.astype(BF16)
    lo = (x - hi.astype(F32)).astype(BF16)
    return hi, lo


def _split3(x):
    t1 = x.astype(BF16)
    r1 = x - t1.astype(F32)
    t2 = r1.astype(BF16)
    t3 = (r1 - t2.astype(F32)).astype(BF16)
    return t1, t2, t3


def _rms_scale(x):
    return lax.rsqrt(jnp.mean(x * x, axis=-1, keepdims=True) + EPS)


def _head_rms_scale(x, gmat):
    x2 = x * x
    hi, lo = _split2(x2)
    outs = []
    for c in range(x.shape[1] // MXU_DIM):
        sl = slice(c * MXU_DIM, (c + 1) * MXU_DIM)
        outs.append(_dot(hi[:, sl], gmat) + _dot(lo[:, sl], gmat))
    ss = outs[0] if len(outs) == 1 else jnp.concatenate(outs, axis=1)
    return lax.rsqrt(ss * (1.0 / HEAD_DIM) + EPS)


def _log_sigmoid(z):
    return jnp.minimum(z, 0.0) - jnp.log1p(jnp.exp(-jnp.abs(z)))


def _gated(a, z):
    ha = 0.5 * a
    return ha + ha * jnp.tanh(0.5 * z)


def _silu(z):
    return _gated(z, z)


def _mem_kernel(mem_ref, g_ref, wkv_ref, ckg_ref, gmat_ref, ck_ref, v_ref):
    x = mem_ref[0]
    h = (x * _rms_scale(x) * g_ref[...]).astype(BF16)
    kv = _dot(h, wkv_ref[...])
    k = kv[:, :CROSS_WIDTH]
    v_ref[0] = kv[:, CROSS_WIDTH:].astype(BF16)
    kn = k * _head_rms_scale(k, gmat_ref[...])
    for i in range(ck_ref.shape[0]):
        ck_ref[i, 0] = (kn * ckg_ref[i]).astype(BF16)


def _mem_call(mem, mem_norm_g, wkv, ckg, gmat):
    b, m, _ = mem.shape
    depth = ckg.shape[0]
    return pl.pallas_call(
        _mem_kernel,
        grid=(b,),
        in_specs=[
            pl.BlockSpec((1, m, D_MODEL), lambda i: (i, 0, 0)),
            pl.BlockSpec((1, D_MODEL), lambda i: (0, 0)),
            pl.BlockSpec((D_MODEL, 2 * CROSS_WIDTH), lambda i: (0, 0)),
            pl.BlockSpec((depth, 1, CROSS_WIDTH), lambda i: (0, 0, 0)),
            pl.BlockSpec((MXU_DIM, MXU_DIM), lambda i: (0, 0)),
        ],
        out_specs=[
            pl.BlockSpec((depth, 1, m, CROSS_WIDTH), lambda i: (0, i, 0, 0)),
            pl.BlockSpec((1, m, CROSS_WIDTH), lambda i: (i, 0, 0)),
        ],
        out_shape=[
            jax.ShapeDtypeStruct((depth, b, m, CROSS_WIDTH), BF16),
            jax.ShapeDtypeStruct((b, m, CROSS_WIDTH), BF16),
        ],
        compiler_params=pltpu.CompilerParams(
            dimension_semantics=("arbitrary",), vmem_limit_bytes=VMEM_LIMIT),
        name="mem_kv",
    )(mem, mem_norm_g, wkv, ckg, gmat)


def _fox_in_kernel(x_ref, g_ref, wqk_ref, wvt_ref, wfc_ref, bf_ref, qg_ref, kg_ref, cqg_ref,
                   gmat_ref, tri_ref, pq_ref, pk_ref, oneq_ref, onek_ref,
                   qa_ref, ka_ref, vt_ref, cq_ref, carry_ref, *, blocks_per_seq):
    i = pl.program_id(0)

    @pl.when(i % blocks_per_seq == 0)
    def _():
        carry_ref[...] = jnp.zeros_like(carry_ref)

    x = x_ref[...]
    tm = x.shape[0]
    h = (x * _rms_scale(x) * g_ref[...]).astype(BF16)
    gmat = gmat_ref[...]

    q = _dot(h, wqk_ref[:, 0:TOKEN_WIDTH])
    qn = (q * _head_rms_scale(q, gmat) * qg_ref[...]).astype(BF16)
    k = _dot(h, wqk_ref[:, TOKEN_WIDTH:2 * TOKEN_WIDTH])
    kn = (k * _head_rms_scale(k, gmat) * kg_ref[...]).astype(BF16)
    vt = _dot_nt(wvt_ref[...], h).astype(BF16)
    ones = jnp.ones((VT_ROWS - HEAD_DIM, tm), BF16)
    for hd in range(FOX_HEADS):
        vt_ref[0, hd * VT_ROWS:hd * VT_ROWS + HEAD_DIM, :] = vt[hd * HEAD_DIM:(hd + 1) * HEAD_DIM, :]
        vt_ref[0, hd * VT_ROWS + HEAD_DIM:(hd + 1) * VT_ROWS, :] = ones

    fc = _dot(h, wfc_ref[...])
    cq = fc[:, :CROSS_WIDTH]
    cq_ref[...] = (cq * _head_rms_scale(cq, gmat) * cqg_ref[...]).astype(BF16)

    lf = _log_sigmoid(fc[:, CROSS_WIDTH:] + bf_ref[...]) * LOG2E
    tri = tri_ref[...]
    carry = carry_ref[0:1, :]
    cs = []
    for sb in range(tm // CUMSUM_BLOCK):
        t1, t2, t3 = _split3(lf[sb * CUMSUM_BLOCK:(sb + 1) * CUMSUM_BLOCK])
        c_sb = (_dot(tri, t1) + _dot(tri, t2)) + _dot(tri, t3) + carry
        carry = c_sb[CUMSUM_BLOCK - 1:CUMSUM_BLOCK, :]
        cs.append(c_sb)
    carry_ref[0:1, :] = carry
    c = jnp.concatenate(cs, axis=0)
    t1, t2, t3 = _split3(c)
    grp = lax.broadcasted_iota(jnp.int32, (1, LANES), 1) // FOX_HEADS
    c3 = jnp.where(grp == 0, t1, jnp.where(grp == 1, t2, t3))
    qext = (_dot(c3, pq_ref[...]) + oneq_ref[...]).astype(BF16)
    kext = (_dot(c3, pk_ref[...]) + onek_ref[...]).astype(BF16)

    for p in range(HEAD_PAIRS):
        src = slice(p * LANES, (p + 1) * LANES)
        qa_ref[:, p * PAIR_AUG:p * PAIR_AUG + LANES] = qn[:, src]
        qa_ref[:, p * PAIR_AUG + LANES:(p + 1) * PAIR_AUG] = qext[:, src]
        ka_ref[:, p * PAIR_AUG:p * PAIR_AUG + LANES] = kn[:, src]
        ka_ref[:, p * PAIR_AUG + LANES:(p + 1) * PAIR_AUG] = kext[:, src]


def _bias_lane_constants():
    pq = np.zeros((LANES, TOKEN_WIDTH), np.float32)
    pk = np.zeros((LANES, TOKEN_WIDTH), np.float32)
    oneq = np.zeros((1, TOKEN_WIDTH), np.float32)
    onek = np.zeros((1, TOKEN_WIDTH), np.float32)
    for p in range(HEAD_PAIRS):
        for half in range(2):
            head = 2 * p + half
            base = p * LANES + 6 * half
            for t in range(3):
                pq[t * FOX_HEADS + head, base + t] = 1.0
                pk[t * FOX_HEADS + head, base + 3 + t] = -1.0
                oneq[0, base + 3 + t] = 1.0
                onek[0, base + t] = 1.0
    return pq, pk, oneq, onek


def _const_spec(shape):
    nd = len(shape)
    return pl.BlockSpec(shape, lambda i: (0,) * nd)


def _fox_in_call(x2, g, wqk, wvt, wfc, bf, qg, kg, cqg, gmat, seq):
    n, _ = x2.shape
    tm = ROW_BLOCK
    bps = seq // tm
    pq, pk, oneq, onek = _bias_lane_constants()
    tri = np.tril(np.ones((CUMSUM_BLOCK, CUMSUM_BLOCK), np.float32))
    consts = [jnp.asarray(tri, BF16), jnp.asarray(pq, BF16), jnp.asarray(pk, BF16),
              jnp.asarray(oneq, F32), jnp.asarray(onek, F32)]
    ins = [x2, g, wqk, wvt, wfc, bf, qg, kg, cqg, gmat] + consts
    in_specs = [pl.BlockSpec((tm, D_MODEL), lambda i: (i, 0))]
    in_specs += [_const_spec(a.shape) for a in ins[1:]]
    return pl.pallas_call(
        functools.partial(_fox_in_kernel, blocks_per_seq=bps),
        grid=(n // tm,),
        in_specs=in_specs,
        out_specs=[
            pl.BlockSpec((tm, HEAD_PAIRS * PAIR_AUG), lambda i: (i, 0)),
            pl.BlockSpec((tm, HEAD_PAIRS * PAIR_AUG), lambda i: (i, 0)),
            pl.BlockSpec((1, FOX_HEADS * VT_ROWS, tm), lambda i: (i // bps, 0, i % bps)),
            pl.BlockSpec((tm, CROSS_WIDTH), lambda i: (i, 0)),
        ],
        out_shape=[
            jax.ShapeDtypeStruct((n, HEAD_PAIRS * PAIR_AUG), BF16),
            jax.ShapeDtypeStruct((n, HEAD_PAIRS * PAIR_AUG), BF16),
            jax.ShapeDtypeStruct((n // seq, FOX_HEADS * VT_ROWS, seq), BF16),
            jax.ShapeDtypeStruct((n, CROSS_WIDTH), BF16),
        ],
        scratch_shapes=[pltpu.VMEM((SUBLANES, LANES), F32)],
        compiler_params=pltpu.CompilerParams(
            dimension_semantics=("arbitrary",), vmem_limit_bytes=VMEM_LIMIT),
        name="fox_in",
    )(*ins)


def _fox_attn_kernel(q_ref, k_ref, vt_ref, o_ref, s_ref, mx_ref, acc_ref, *, tq, tk):
    feat = lax.broadcasted_iota(jnp.int32, (PAIR_AUG, 1), 0)
    keep_a = (feat < HEAD_DIM) | ((feat >= LANES) & (feat < LANES + 6))
    keep_b = ((feat >= HEAD_DIM) & (feat < LANES)) | ((feat >= LANES + 6) & (feat < LANES + 12))
    nsub = tq // tk
    tri_bias = jnp.where(lax.broadcasted_iota(jnp.int32, (tk, tk), 0)
                         <= lax.broadcasted_iota(jnp.int32, (tk, tk), 1), 0.0, NEG).astype(F32)

    def q_block(i):
        q = q_ref[0, i * tq:(i + 1) * tq, :]
        qt = jnp.transpose(q.astype(F32))
        qt_heads = tuple(jnp.where(keep, qt, 0.0).astype(BF16) for keep in (keep_a, keep_b))
        acc_ref[...] = jnp.zeros_like(acc_ref)

        def scores(j, slot, c0=0, want_max=True):
            start = j * tk if isinstance(j, int) else pl.multiple_of(j * tk, tk)
            ks = k_ref[0, pl.ds(start, tk), :]
            for h in range(2):
                s = _dot(ks, qt_heads[h][:, c0:])
                s_ref[slot, h, :, c0:] = s
                if want_max:
                    mx_ref[slot, h, 0:1, :] = jnp.max(s, axis=0, keepdims=True)

        def update(j, slot, ms, c0=0, masked=False):
            start = j * tk if isinstance(j, int) else pl.multiple_of(j * tk, tk)
            vts = vt_ref[0, :, pl.ds(start, tk)]
            out = []
            for h in range(2):
                m_old = ms[h]
                s = s_ref[slot, h, :, c0:]
                if masked:
                    tile = s[:, :tk] + tri_bias
                    s = tile if s.shape[1] == tk else jnp.concatenate([tile, s[:, tk:]], axis=1)
                    s_max = jnp.max(s, axis=0, keepdims=True)
                else:
                    s_max = mx_ref[slot, h, 0:1, :]
                m_new = jnp.maximum(m_old, s_max)
                alpha = jnp.exp2(m_old - m_new)
                p = jnp.exp2(s - m_new).astype(BF16)
                rows = slice(h * VT_ROWS, (h + 1) * VT_ROWS)
                acc_ref[rows, c0:] = alpha * acc_ref[rows, c0:] + _dot(vts[rows, :], p)
                out.append(m_new)
            return tuple(out)

        scores(0, 0)

        def trip(t, ms):
            for u in range(nsub):
                blk = nsub * t + u
                scores(blk + 1, (u + 1) % 2)
                ms = update(blk, u % 2, ms)
            return ms

        ms = tuple(jnp.full((1, tq), NEG, F32) for _ in range(2))
        if i > 0:
            ms = lax.fori_loop(0, i, trip, ms)
        for d in range(nsub):
            if d + 1 < nsub:
                scores(nsub * i + d + 1, (d + 1) % 2, c0=(d + 1) * tk, want_max=False)
            ms = update(nsub * i + d, d % 2, ms, c0=d * tk, masked=True)
            if d + 1 < nsub:
                ms = tuple(m[:, tk:] for m in ms)
        outs = []
        for h in range(2):
            blk = acc_ref[h * VT_ROWS:(h + 1) * VT_ROWS, :]
            outs.append(blk[:HEAD_DIM, :] * (1.0 / blk[HEAD_DIM:HEAD_DIM + 1, :]))
        o_ref[0, i * tq:(i + 1) * tq, :] = jnp.transpose(jnp.concatenate(outs, axis=0)).astype(BF16)

    for i in range(q_ref.shape[1] // tq):
        q_block(i)


def _fox_attn_call(qa, ka, vt):
    b, _, s = vt.shape
    tq = ATTN_BLOCK
    tk = ATTN_KV_BLOCK
    return pl.pallas_call(
        functools.partial(_fox_attn_kernel, tq=tq, tk=tk),
        grid=(b, HEAD_PAIRS),
        in_specs=[
            pl.BlockSpec((1, s, PAIR_AUG), lambda bi, p: (bi, 0, p)),
            pl.BlockSpec((1, s, PAIR_AUG), lambda bi, p: (bi, 0, p)),
            pl.BlockSpec((1, 2 * VT_ROWS, s), lambda bi, p: (bi, p, 0)),
        ],
        out_specs=pl.BlockSpec((1, s, LANES), lambda bi, p: (bi, 0, p)),
        out_shape=jax.ShapeDtypeStruct((b, s, TOKEN_WIDTH), BF16),
        scratch_shapes=[
            pltpu.VMEM((2, 2, tk, tq), F32),
            pltpu.VMEM((2, 2, SUBLANES, tq), F32),
            pltpu.VMEM((2 * VT_ROWS, tq), F32),
        ],
        compiler_params=pltpu.CompilerParams(
            dimension_semantics=("arbitrary", "arbitrary"),
            vmem_limit_bytes=VMEM_LIMIT),
        name="fox_attn",
    )(qa, ka, vt)


def _cross_attention(cq, ck, vm):
    is_a = lax.broadcasted_iota(jnp.int32, (1, LANES), 1) < HEAD_DIM
    outs = []
    for p in range(CROSS_WIDTH // LANES):
        sl = slice(p * LANES, (p + 1) * LANES)
        cqp, ckp, vp = cq[:, sl], ck[:, sl], vm[:, sl]
        zero = jnp.zeros_like(cqp)
        halves = []
        for keep in (is_a, jnp.logical_not(is_a)):
            s = _dot_nt(jnp.where(keep, cqp, zero), ckp)
            m = jnp.max(s, axis=-1, keepdims=True)
            pr = jnp.exp2(s - m)
            l = jnp.sum(pr, axis=-1, keepdims=True)
            halves.append(_dot(pr.astype(BF16), vp) * (1.0 / l))
        outs.append(jnp.where(is_a, halves[0], halves[1]))
    return jnp.concatenate(outs, axis=1).astype(BF16)


def _mix_out_fox_kernel(tok_ref, cq_ref, ck_ref, vm_ref, wout_ref, x_ref, o_ref):
    cross = _cross_attention(cq_ref[...], ck_ref[0, 0], vm_ref[0])
    y = _dot(tok_ref[...], wout_ref[0:TOKEN_WIDTH, :]) + _dot(cross, wout_ref[TOKEN_WIDTH:, :])
    o_ref[...] = x_ref[...] + y


def _mix_out_conv_kernel(u_ref, halo_ref, dw_ref, dwb_ref, lng_ref, lnb_ref,
                         cq_ref, ck_ref, vm_ref, wout_ref, x_ref, o_ref, ubuf_ref, y_ref, *, blocks_per_seq):
    i = pl.program_id(0)
    tm = u_ref.shape[0]
    first = i % blocks_per_seq == 0
    halo = halo_ref[...]
    ubuf_ref[0:CONV_HALO, :] = jnp.where(first, jnp.zeros_like(halo), halo)
    ubuf_ref[CONV_HALO:CONV_HALO + tm, :] = u_ref[...]

    off = CONV_HALO - (CONV_WIDTH - 1)

    def conv_rows(rb, _):
        r0 = pl.multiple_of(rb * CONV_ROWS, CONV_ROWS)
        for c in range(TOKEN_WIDTH // LANES):
            sl = slice(c * LANES, (c + 1) * LANES)
            y_blk = None
            for res in range(SUBLANES):
                ext = CONV_ROWS + (SUBLANES if res else 0)
                z = None
                for k in range(CONV_WIDTH):
                    if (off + k) % SUBLANES != res:
                        continue
                    term = dw_ref[k:k + 1, sl] * ubuf_ref[pl.ds(r0 + (off + k - res), ext), sl]
                    z = term if z is None else z + term
                part = z[res:res + CONV_ROWS, :]
                y_blk = part if y_blk is None else y_blk + part
            y_ref[pl.ds(r0, CONV_ROWS), sl] = y_blk + dwb_ref[:, sl]
        return 0

    lax.fori_loop(0, tm // CONV_ROWS, conv_rows, 0)
    y = y_ref[...]

    mu = jnp.mean(y, axis=-1, keepdims=True)
    yc = y - mu
    yn = yc * lax.rsqrt(jnp.mean(yc * yc, axis=-1, keepdims=True) + EPS)
    tok = _silu(yn * lng_ref[...] + lnb_ref[...]).astype(BF16)

    cross = _cross_attention(cq_ref[...], ck_ref[0, 0], vm_ref[0])
    out = _dot(tok, wout_ref[0:TOKEN_WIDTH, :]) + _dot(cross, wout_ref[TOKEN_WIDTH:, :])
    o_ref[...] = x_ref[...] + out


def _mix_out_call(layer, tok_or_u, cq, ck, vm, wout, x2, seq, conv_params=None):
    n, _ = x2.shape
    tm = ROW_BLOCK
    bps = seq // tm
    m = ck.shape[2]
    row = lambda w: pl.BlockSpec((tm, w), lambda i: (i, 0))
    tail_specs = [
        row(CROSS_WIDTH),
        pl.BlockSpec((1, 1, m, CROSS_WIDTH), lambda i: (layer, i // bps, 0, 0)),
        pl.BlockSpec((1, m, CROSS_WIDTH), lambda i: (i // bps, 0, 0)),
        _layer_spec(wout.shape, layer),
        row(D_MODEL),
    ]
    params = pltpu.CompilerParams(dimension_semantics=("arbitrary",), vmem_limit_bytes=VMEM_LIMIT)
    out_shape = jax.ShapeDtypeStruct((n, D_MODEL), F32)
    if conv_params is None:
        return pl.pallas_call(
            _mix_out_fox_kernel,
            grid=(n // tm,),
            in_specs=[row(TOKEN_WIDTH)] + tail_specs,
            out_specs=row(D_MODEL),
            out_shape=out_shape,
            compiler_params=params,
            name="mix_out_fox",
        )(tok_or_u, cq, ck, vm, wout, x2)
    dw, dwb, lng, lnb = conv_params
    halo_blocks = tm // CONV_HALO
    halo_spec = pl.BlockSpec((CONV_HALO, TOKEN_WIDTH),
                             lambda i: (jnp.maximum(i * halo_blocks - 1, 0), 0))
    return pl.pallas_call(
        functools.partial(_mix_out_conv_kernel, blocks_per_seq=bps),
        grid=(n // tm,),
        in_specs=[row(TOKEN_WIDTH), halo_spec] + [_const_spec(a.shape) for a in conv_params] + tail_specs,
        out_specs=row(D_MODEL),
        out_shape=out_shape,
        scratch_shapes=[pltpu.VMEM((CONV_HALO + tm, TOKEN_WIDTH), F32),
                        pltpu.VMEM((tm, TOKEN_WIDTH), F32)],
        compiler_params=params,
        name="mix_out_conv",
    )(tok_or_u, tok_or_u, dw, dwb, lng, lnb, cq, ck, vm, wout, x2)


def _conv_in_kernel(x_ref, g_ref, w_ref, cqg_ref, gmat_ref, u_ref, cq_ref):
    x = x_ref[...]
    h = (x * _rms_scale(x) * g_ref[...]).astype(BF16)
    a = _dot(h, w_ref[:, 0:TOKEN_WIDTH])
    gate = _dot(h, w_ref[:, TOKEN_WIDTH:2 * TOKEN_WIDTH])
    u_ref[...] = _gated(a, gate)
    cq = _dot(h, w_ref[:, 2 * TOKEN_WIDTH:])
    cq_ref[...] = (cq * _head_rms_scale(cq, gmat_ref[...]) * cqg_ref[...]).astype(BF16)


def _conv_in_call(x2, g, w, cqg, gmat):
    n, _ = x2.shape
    tm = ROW_BLOCK
    ins = [x2, g, w, cqg, gmat]
    return pl.pallas_call(
        _conv_in_kernel,
        grid=(n // tm,),
        in_specs=[pl.BlockSpec((tm, D_MODEL), lambda i: (i, 0))] + [_const_spec(a.shape) for a in ins[1:]],
        out_specs=[
            pl.BlockSpec((tm, TOKEN_WIDTH), lambda i: (i, 0)),
            pl.BlockSpec((tm, CROSS_WIDTH), lambda i: (i, 0)),
        ],
        out_shape=[
            jax.ShapeDtypeStruct((n, TOKEN_WIDTH), F32),
            jax.ShapeDtypeStruct((n, CROSS_WIDTH), BF16),
        ],
        compiler_params=pltpu.CompilerParams(
            dimension_semantics=("arbitrary",), vmem_limit_bytes=VMEM_LIMIT),
        name="conv_in",
    )(*ins)


def _ffn_kernel(x_ref, g_ref, wup_ref, conv_ref, wd_ref, o_ref,
                carry_ref, acc_ref, ubuf_ref, *, blocks_per_seq):
    i = pl.program_id(0)
    first = i % blocks_per_seq == 0
    x = x_ref[...]
    tm = x.shape[0]
    n_chunks = FFN_DIM // FFN_CHUNK
    h = (x * _rms_scale(x) * g_ref[...]).astype(BF16)

    def cols(c, k):
        return slice(k * FFN_DIM + c * FFN_CHUNK, k * FFN_DIM + (c + 1) * FFN_CHUNK)

    def up_project(c):
        slot = c % 2
        for k in range(2):
            u = _dot(h, wup_ref[:, cols(c, k)])
            tail = carry_ref[k, c]
            ubuf_ref[slot, k, 0:SUBLANES, :] = jnp.where(first, jnp.zeros_like(tail), tail)
            ubuf_ref[slot, k, SUBLANES:SUBLANES + tm, :] = u
            carry_ref[k, c] = u[tm - SUBLANES:, :]

    def causal_conv3(c, k):
        slot = c % 2
        w = conv_ref[:, cols(c, k)]
        taps = [ubuf_ref[slot, k, SUBLANES - 2 + j:SUBLANES - 2 + j + tm, :] for j in range(FFN_CONV_WIDTH)]
        return w[0:1, :] * taps[0] + w[1:2, :] * taps[1] + w[2:3, :] * taps[2]

    up_project(0)
    for c in range(n_chunks):
        if c + 1 < n_chunks:
            up_project(c + 1)
        act = (_silu(causal_conv3(c, 1)) * causal_conv3(c, 0)).astype(BF16)
        y = _dot(act, wd_ref[c * FFN_CHUNK:(c + 1) * FFN_CHUNK, :])
        if c == 0:
            acc_ref[...] = y
        else:
            acc_ref[...] += y
    o_ref[...] = x + acc_ref[...]


def _layer_spec(shape, layer):
    return pl.BlockSpec((None,) + tuple(shape[1:]), lambda i: (layer, 0, 0))


def _ffn_call(layer, x2, g, wup, conv, wd, seq):
    n, _ = x2.shape
    tm = FFN_ROW_BLOCK
    n_chunks = FFN_DIM // FFN_CHUNK
    ins = [x2, g, wup, conv, wd]
    return pl.pallas_call(
        functools.partial(_ffn_kernel, blocks_per_seq=seq // tm),
        grid=(n // tm,),
        in_specs=[pl.BlockSpec((tm, D_MODEL), lambda i: (i, 0)), _const_spec(g.shape)]
        + [_layer_spec(a.shape, layer) for a in (wup, conv, wd)],
        out_specs=pl.BlockSpec((tm, D_MODEL), lambda i: (i, 0)),
        out_shape=jax.ShapeDtypeStruct((n, D_MODEL), F32),
        scratch_shapes=[
            pltpu.VMEM((2, n_chunks, SUBLANES, FFN_CHUNK), F32),
            pltpu.VMEM((tm, D_MODEL), F32),
            pltpu.VMEM((2, 2, SUBLANES + tm, FFN_CHUNK), F32),
        ],
        compiler_params=pltpu.CompilerParams(
            dimension_semantics=("arbitrary",), vmem_limit_bytes=VMEM_LIMIT),
        name="ffn",
    )(*ins)


def _head_tiled(g, heads, scale=1.0):
    return (jnp.tile(g.astype(F32), heads) * scale).reshape(1, heads * HEAD_DIM)


def kernel(x, mem, mem_norm_g, mem_w_kv, mix_norm_g, mix_w_out, cross_q_g, cross_k_g,
           fox_w_in, fox_b_f, fox_q_g, fox_k_g,
           conv_w_in, conv_dw, conv_dw_b, conv_ln_g, conv_ln_b,
           ffn_norm_g, ffn_w_up, ffn_conv, ffn_w_down):
    b, s, d = x.shape
    depth = mix_norm_g.shape[0]
    assert d == D_MODEL and s % ROW_BLOCK == 0 and s % ATTN_BLOCK == 0
    assert ATTN_BLOCK % (2 * ATTN_KV_BLOCK) == 0
    assert ROW_BLOCK % CUMSUM_BLOCK == 0 and ROW_BLOCK % CONV_HALO == 0 and ROW_BLOCK % CONV_ROWS == 0

    gmat = jnp.asarray(np.kron(np.eye(MXU_DIM // HEAD_DIM), np.ones((HEAD_DIM, HEAD_DIM))), BF16)
    ckg = jnp.stack([_head_tiled(cross_k_g[i], CROSS_HEADS) for i in range(depth)])
    ck, vm = _mem_call(mem, mem_norm_g.reshape(1, d), mem_w_kv.astype(BF16), ckg, gmat)

    wout = mix_w_out.astype(BF16)
    wup_all = ffn_w_up.astype(BF16)
    wd_all = ffn_w_down.astype(BF16)
    x2 = x.reshape(b * s, d)
    for i in range(depth):
        j = i // 2
        g = mix_norm_g[i].reshape(1, d)
        cqg = _head_tiled(cross_q_g[i], CROSS_HEADS, QK_SCALE * LOG2E)
        if i % 2 == 0:
            w = fox_w_in[j]
            wqk = w[:, :2 * TOKEN_WIDTH].astype(BF16)
            wvt = w[:, 2 * TOKEN_WIDTH:3 * TOKEN_WIDTH].T.astype(BF16)
            wf = jnp.tile(w[:, 3 * TOKEN_WIDTH:3 * TOKEN_WIDTH + FOX_HEADS], (1, 3))
            wf = jnp.pad(wf, ((0, 0), (0, LANES - 3 * FOX_HEADS)))
            wfc = jnp.concatenate([w[:, 3 * TOKEN_WIDTH + FOX_HEADS:], wf], axis=1).astype(BF16)
            bf = jnp.pad(jnp.tile(fox_b_f[j].astype(F32), 3), (0, LANES - 3 * FOX_HEADS)).reshape(1, LANES)
            qg = _head_tiled(fox_q_g[j], FOX_HEADS, QK_SCALE * LOG2E)
            kg = _head_tiled(fox_k_g[j], FOX_HEADS)
            qa, ka, vt, cq = _fox_in_call(x2, g, wqk, wvt, wfc, bf, qg, kg, cqg, gmat, s)
            tok = _fox_attn_call(qa.reshape(b, s, -1), ka.reshape(b, s, -1), vt)
            x2 = _mix_out_call(i, tok.reshape(b * s, -1), cq, ck, vm, wout, x2, s)
        else:
            u, cq = _conv_in_call(x2, g, conv_w_in[j].astype(BF16), cqg, gmat)
            conv_params = (conv_dw[j].astype(F32), conv_dw_b[j].reshape(1, -1).astype(F32),
                           conv_ln_g[j].reshape(1, -1).astype(F32), conv_ln_b[j].reshape(1, -1).astype(F32))
            x2 = _mix_out_call(i, u, cq, ck, vm, wout, x2, s, conv_params)
        x2 = _ffn_call(i, x2, ffn_norm_g[i].reshape(1, d), wup_all, ffn_conv.astype(F32), wd_all, s)
    return x2.reshape(b, s, d)
```
